```python
import math
import jax
import jax.numpy as jnp
from jax import lax
import numpy as np

D_MODEL = 2048
BATCH = 1
SEQ = 16384
DEPTH = 4

CTX_LEN = 256
GRID_W = 64
EPS = 1e-6

RG_WIDTH = D_MODEL // 2
RG_BLOCKS = 8
RG_BLOCK = RG_WIDTH // RG_BLOCKS
RG_C = 8.0
CONV_W = 4
CONV_PAD = (CONV_W // 2, CONV_W - 1 - CONV_W // 2)

NA_HEAD_DIM = 128
NA_HEADS = D_MODEL // 4 // NA_HEAD_DIM
NA_WIN_H = 8
NA_WIN_W = 16
NA_SCALE = NA_HEAD_DIM ** -0.5

DA_HEAD_DIM = 64
DA_V_DIM = 2 * DA_HEAD_DIM
DA_HEADS = D_MODEL // 4 // DA_V_DIM
DA_SCALE = DA_HEAD_DIM ** -0.5
Q_BLOCK = 128
ROPE_THETA = 10000.0

NA_WIDTH = NA_HEADS * NA_HEAD_DIM
DA_QK_WIDTH = DA_HEADS * 2 * DA_HEAD_DIM
DA_WIDTH = DA_HEADS * DA_V_DIM
MIX_WIDTH = RG_WIDTH + NA_WIDTH + DA_WIDTH
IN_SPLITS = (RG_WIDTH, RG_WIDTH, NA_WIDTH, NA_WIDTH, NA_WIDTH, DA_QK_WIDTH, DA_QK_WIDTH, DA_WIDTH)
IN_WIDTH = sum(IN_SPLITS)

N_EXPERTS = 16
N_GROUPS = 4
GROUP_SIZE = N_EXPERTS // N_GROUPS
TOP_K = 2
D_EXPERT = D_MODEL // 2
MOE_BLOCK = 256

kernel_name = 'hybrid_rglru_natten_diffattn_groupmoe_dit'


def rms_norm(x, g):
    xf = x.astype(jnp.float32)
    y = xf * lax.rsqrt(jnp.mean(xf * xf, axis=-1, keepdims=True) + EPS)
    return (y * g.astype(jnp.float32)).astype(x.dtype)


def modulate(h, shift, scale):
    return h * (1 + scale) + shift


def centred_dwconv(u, w, b):
    y = lax.conv_general_dilated(u, w.astype(u.dtype)[:, None, :], window_strides=(1,),
                                 padding=[CONV_PAD], dimension_numbers=('NWC', 'WIO', 'NWC'),
                                 feature_group_count=u.shape[-1])
    return y + b.astype(u.dtype)


def block_diag_linear(u, w, b):
    bsz, length, width = u.shape
    y = jnp.einsum('blnk,nkj->blnj', u.reshape(bsz, length, RG_BLOCKS, RG_BLOCK), w)
    return y.reshape(bsz, length, width) + b


def linear_scan(a, b, h0):
    def combine(left, right):
        a_l, b_l = left
        a_r, b_r = right
        return a_l * a_r, a_r * b_l + b_r
    a_cum, b_cum = lax.associative_scan(combine, (a, b), axis=1)
    return a_cum * h0[:, None, :] + b_cum


def rg_lru_scan(u, gate_w, gate_b, lam, h0, reverse):
    f32 = jnp.float32
    uf = u.astype(f32)
    r = jax.nn.sigmoid(block_diag_linear(uf, gate_w[0].astype(f32), gate_b[0].astype(f32)))
    i = jax.nn.sigmoid(block_diag_linear(uf, gate_w[1].astype(f32), gate_b[1].astype(f32)))
    log_a = -RG_C * r * jax.nn.softplus(-lam.astype(f32))
    a = jnp.exp(log_a)
    b = jnp.sqrt(-jnp.expm1(2.0 * log_a)) * (i * uf)
    if reverse:
        a, b = jnp.flip(a, 1), jnp.flip(b, 1)
    h = linear_scan(a, b, h0)
    h_last = h[:, -1]
    if reverse:
        h = jnp.flip(h, 1)
    return h, h_last


def recurrent_group(ux, gx, uc, gc, conv_w, conv_b, gate_w, gate_b, lam, need_ctx):
    ux = centred_dwconv(ux, conv_w, conv_b)
    uc = centred_dwconv(uc, conv_w, conv_b)
    h0 = jnp.zeros((uc.shape[0], uc.shape[-1]), jnp.float32)
    hx_dirs = []
    hc_dirs = []
    for d in range(2):
        hc, hc_last = rg_lru_scan(uc, gate_w[d], gate_b[d], lam[d], h0, d == 1)
        hx, _ = rg_lru_scan(ux, gate_w[d], gate_b[d], lam[d], hc_last, d == 1)
        hx_dirs.append(hx)
        hc_dirs.append(hc)
    ox = ((hx_dirs[0] + hx_dirs[1]) * jax.nn.gelu(gx.astype(jnp.float32))).astype(gx.dtype)
    oc = None
    if need_ctx:
        oc = ((hc_dirs[0] + hc_dirs[1]) * jax.nn.gelu(gc.astype(jnp.float32))).astype(gc.dtype)
    return ox, oc


def neighbourhood_attention(q, k, v, k_ctx, v_ctx, rpb):
    bsz, seq, heads, dh = q.shape
    rows = seq // GRID_W
    kh = min(NA_WIN_H, rows)
    kw = NA_WIN_W
    qg = q.reshape(bsz, rows, GRID_W, heads, dh)
    kg = k.reshape(bsz, rows, GRID_W, heads, dh)
    vg = v.reshape(bsz, rows, GRID_W, heads, dh)
    col = jnp.arange(GRID_W)
    cidx = jnp.clip(col - kw // 2, 0, GRID_W - kw)[:, None] + jnp.arange(kw)[None, :]
    rpb_cols = rpb[:, :, cidx - col[:, None] + (NA_WIN_W - 1)]

    def row_block(r):
        rs = jnp.clip(r - kh // 2, 0, rows - kh)
        k_nb = lax.dynamic_slice_in_dim(kg, rs, kh, axis=1)[:, :, cidx]
        v_nb = lax.dynamic_slice_in_dim(vg, rs, kh, axis=1)[:, :, cidx]
        q_r = lax.dynamic_index_in_dim(qg, r, axis=1, keepdims=False)
        s_nb = jnp.einsum('bqhd,bjqwhd->bhqjw', q_r, k_nb).astype(jnp.float32) * NA_SCALE
        row_idx = rs + jnp.arange(kh) - r + (NA_WIN_H - 1)
        bias = jnp.transpose(rpb_cols[:, row_idx], (0, 2, 1, 3))
        s_nb = s_nb + bias.astype(jnp.float32)
        s_c = jnp.einsum('bqhd,bkhd->bhqk', q_r, k_ctx).astype(jnp.float32) * NA_SCALE
        p = jax.nn.softmax(jnp.concatenate([s_nb.reshape(bsz, heads, GRID_W, kh * kw), s_c], -1), axis=-1)
        p_nb = p[..., :kh * kw].reshape(bsz, heads, GRID_W, kh, kw).astype(v.dtype)
        p_c = p[..., kh * kw:].astype(v.dtype)
        return (jnp.einsum('bhqjw,bjqwhd->bqhd', p_nb, v_nb)
                + jnp.einsum('bhqk,bkhd->bqhd', p_c, v_ctx))

    o = lax.map(row_block, jnp.arange(rows))
    return jnp.moveaxis(o, 0, 1).reshape(bsz, seq, heads, dh)


def context_attention(q, k, v):
    s = jnp.einsum('bqhd,bkhd->bhqk', q, k).astype(jnp.float32) * NA_SCALE
    p = jax.nn.softmax(s, axis=-1).astype(v.dtype)
    return jnp.einsum('bhqk,bkhd->bqhd', p, v)


def rope_2d_tables(length):
    t = jnp.arange(length, dtype=jnp.int32)
    n = DA_HEAD_DIM // 4
    inv = 1.0 / (ROPE_THETA ** (jnp.arange(n, dtype=jnp.float32) / n))
    ang_r = (t // GRID_W).astype(jnp.float32)[:, None] * inv
    ang_c = (t % GRID_W).astype(jnp.float32)[:, None] * inv
    return jnp.cos(ang_r), jnp.sin(ang_r), jnp.cos(ang_c), jnp.sin(ang_c)


def _rotate(x, cos, sin):
    cos = cos[None, :, None, None, :]
    sin = sin[None, :, None, None, :]
    x1, x2 = jnp.split(x, 2, axis=-1)
    return jnp.concatenate([x1 * cos - x2 * sin, x2 * cos + x1 * sin], axis=-1)


def apply_rope_2d(x, tabs):
    cr, sr, cc, sc = tabs
    xr, xc = jnp.split(x.astype(jnp.float32), 2, axis=-1)
    return jnp.concatenate([_rotate(xr, cr, sr), _rotate(xc, cc, sc)], axis=-1).astype(x.dtype)


def diff_attention(q, k, v, lam, q_block):
    bsz, length, heads, _, dd = q.shape
    nblk = length // q_block
    qb = jnp.moveaxis(q.reshape(bsz, nblk, q_block, heads, 2, dd), 1, 0)

    def step(qblk):
        s = jnp.einsum('bqhmd,bkhmd->bhmqk', qblk, k).astype(jnp.float32) * DA_SCALE
        p = jax.nn.softmax(s, axis=-1)
        a = p[:, :, 0] - lam * p[:, :, 1]
        return jnp.einsum('bhqk,bkhd->bqhd', a.astype(v.dtype), v)

    o = lax.map(step, qb)
    return jnp.moveaxis(o, 0, 1).reshape(bsz, length, heads, v.shape[-1])


def token_mixers(hx, hc, w_in, conv_w, conv_b, gate_w, gate_b, lam, na_g, rpb,
                 da_g, da_lam, subln_g, lambda_init, rope, need_ctx):
    bsz, seq, _ = hx.shape
    clen = hc.shape[1]
    cuts = [int(v) for v in np.cumsum(IN_SPLITS)[:-1]]
    px = jnp.split(hx @ w_in, cuts, axis=-1)
    pc = jnp.split(hc @ w_in, cuts, axis=-1)

    ra_x, ra_c = recurrent_group(px[0], px[1], pc[0], pc[1], conv_w, conv_b, gate_w, gate_b, lam, need_ctx)

    def na_heads(t, length):
        return t.reshape(bsz, length, NA_HEADS, NA_HEAD_DIM)
    nq_x = rms_norm(na_heads(px[2], seq), na_g[0])
    nk_x = rms_norm(na_heads(px[3], seq), na_g[1])
    nv_x = na_heads(px[4], seq)
    nq_c = rms_norm(na_heads(pc[2], clen), na_g[0])
    nk_c = rms_norm(na_heads(pc[3], clen), na_g[1])
    nv_c = na_heads(pc[4], clen)
    nb_x = neighbourhood_attention(nq_x, nk_x, nv_x, nk_c, nv_c, rpb)

    def da_heads(t, length):
        return t.reshape(bsz, length, DA_HEADS, 2, DA_HEAD_DIM)
    dq_x = apply_rope_2d(rms_norm(da_heads(px[5], seq), da_g[0]), rope)
    dk_x = apply_rope_2d(rms_norm(da_heads(px[6], seq), da_g[1]), rope)
    dv_x = px[7].reshape(bsz, seq, DA_HEADS, DA_V_DIM)
    dq_c = rms_norm(da_heads(pc[5], clen), da_g[0])
    dk_c = rms_norm(da_heads(pc[6], clen), da_g[1])
    dv_c = pc[7].reshape(bsz, clen, DA_HEADS, DA_V_DIM)
    dl = da_lam.astype(jnp.float32)
    lam_full = jnp.exp(jnp.sum(dl[0] * dl[1])) - jnp.exp(jnp.sum(dl[2] * dl[3])) + lambda_init
    k_all = jnp.concatenate([dk_x, dk_c], axis=1)
    v_all = jnp.concatenate([dv_x, dv_c], axis=1)
    da_x = rms_norm(diff_attention(dq_x, k_all, v_all, lam_full, Q_BLOCK), subln_g) * (1 - lambda_init)

    ox = jnp.concatenate([ra_x, nb_x.reshape(bsz, seq, NA_WIDTH), da_x.reshape(bsz, seq, DA_WIDTH)], axis=-1)
    oc = None
    if need_ctx:
        nb_c = context_attention(nq_c, nk_c, nv_c)
        da_c = rms_norm(diff_attention(dq_c, dk_c, dv_c, lam_full, clen), subln_g) * (1 - lambda_init)
        oc = jnp.concatenate([ra_c, nb_c.reshape(bsz, clen, NA_WIDTH), da_c.reshape(bsz, clen, DA_WIDTH)], axis=-1)
    return ox, oc


def route_grouped(h, router_w, router_b):
    n_tok = h.shape[0]
    aff = jax.nn.sigmoid((h @ router_w).astype(jnp.float32))
    sel = aff + router_b.astype(jnp.float32)
    grp = sel.reshape(n_tok, N_GROUPS, GROUP_SIZE)
    grp_score = jnp.sum(lax.top_k(grp, 2)[0], axis=-1)
    best = jnp.argmax(grp_score, axis=-1)
    in_best = (jnp.arange(N_GROUPS)[None, :] == best[:, None])[:, :, None]
    masked = jnp.where(in_best, grp, -jnp.inf).reshape(n_tok, N_EXPERTS)
    _, eidx = lax.top_k(masked, TOP_K)
    w = jnp.take_along_axis(aff, eidx, axis=1)
    return eidx, w / jnp.sum(w, axis=-1, keepdims=True)


def moe_experts(h, eidx, w, w_gate, w_up, w_down):
    n_tok, dm = h.shape
    n_assign = n_tok * TOP_K
    flat_e = eidx.reshape(-1).astype(jnp.int32)
    flat_tok = jnp.arange(n_assign, dtype=jnp.int32) // TOP_K
    flat_w = w.reshape(-1)
    order = jnp.argsort(flat_e)
    se, stok, sw = flat_e[order], flat_tok[order], flat_w[order]
    counts = jax.ops.segment_sum(jnp.ones_like(flat_e), flat_e, num_segments=N_EXPERTS)
    starts = jnp.cumsum(counts) - counts
    pcounts = (counts + MOE_BLOCK - 1) // MOE_BLOCK * MOE_BLOCK
    pends = jnp.cumsum(pcounts)
    pstarts = pends - pcounts
    pos = pstarts[se] + (jnp.arange(n_assign, dtype=jnp.int32) - starts[se])
    n_blocks = -(-n_assign // MOE_BLOCK) + N_EXPERTS
    buf = jnp.zeros((n_blocks * MOE_BLOCK, dm), h.dtype).at[pos].set(h[stok])
    block_e = jnp.minimum(jnp.searchsorted(pends, jnp.arange(n_blocks, dtype=jnp.int32) * MOE_BLOCK,
                                           side='right'), N_EXPERTS - 1)

    def expert_block(args):
        xb, e = args
        return (jax.nn.silu(xb @ w_gate[e]) * (xb @ w_up[e])) @ w_down[e]

    yb = lax.map(expert_block, (buf.reshape(n_blocks, MOE_BLOCK, dm), block_e))
    y = yb.reshape(n_blocks * MOE_BLOCK, dm)[pos]
    return jnp.zeros((n_tok, dm), h.dtype).at[stok].add(y * sw[:, None].astype(y.dtype))


def setup_inputs(seed: int = 0) -> dict:
    key = jax.random.key(seed)
    ks = jax.random.split(key, 26)
    f32 = jnp.float32
    dm = D_MODEL

    def normal(k, shape, scale):
        return scale * jax.random.normal(k, shape, f32)

    u = jax.random.uniform(ks[14], (DEPTH, 2, RG_WIDTH), f32, 0.9, 0.999)
    p = u ** (1.0 / RG_C)
    return {
        'x': normal(ks[0], (BATCH, SEQ, dm), 1.0),
        'c': normal(ks[1], (BATCH, dm), 1.0),
        'ctx': normal(ks[2], (BATCH, CTX_LEN, dm), 1.0),
        'c_ctx': normal(ks[3], (dm,), 1.0),
        'ada_w': normal(ks[4], (DEPTH, dm, 6 * dm), 0.5 * dm ** -0.5),
        'ada_b': normal(ks[5], (DEPTH, 6 * dm), 0.02),
        'norm1_g': 1.0 + normal(ks[6], (DEPTH, dm), 0.02),
        'norm2_g': 1.0 + normal(ks[7], (DEPTH, dm), 0.02),
        'w_in': normal(ks[8], (DEPTH, dm, IN_WIDTH), dm ** -0.5),
        'w_out': normal(ks[9], (DEPTH, MIX_WIDTH, dm), MIX_WIDTH ** -0.5),
        'rg_conv_w': normal(ks[10], (DEPTH, CONV_W, RG_WIDTH), CONV_W ** -0.5),
        'rg_conv_b': normal(ks[11], (DEPTH, RG_WIDTH), 0.02),
        'rg_gate_w': normal(ks[12], (DEPTH, 2, 2, RG_BLOCKS, RG_BLOCK, RG_BLOCK), RG_BLOCK ** -0.5),
        'rg_gate_b': normal(ks[13], (DEPTH, 2, 2, RG_WIDTH), 0.02),
        'rg_lambda': jnp.log(p) - jnp.log1p(-p),
        'na_qk_g': 1.0 + normal(ks[15], (DEPTH, 2, NA_HEAD_DIM), 0.02),
        'na_rpb': normal(ks[16], (DEPTH, NA_HEADS, 2 * NA_WIN_H - 1, 2 * NA_WIN_W - 1), 0.1),
        'da_qk_g': 1.0 + normal(ks[17], (DEPTH, 2, DA_HEAD_DIM), 0.02),
        'da_lambda': normal(ks[18], (DEPTH, 4, DA_HEAD_DIM), 0.1),
        'da_subln_g': 1.0 + normal(ks[19], (DEPTH, DA_V_DIM), 0.02),
        'router_w': normal(ks[20], (dm, N_EXPERTS), dm ** -0.5),
        'router_b': normal(ks[21], (N_EXPERTS,), 0.01),
        'moe_w_gate': normal(ks[22], (DEPTH, N_EXPERTS, dm, D_EXPERT), dm ** -0.5),
        'moe_w_up': normal(ks[23], (DEPTH, N_EXPERTS, dm, D_EXPERT), dm ** -0.5),
        'moe_w_down': normal(ks[24], (DEPTH, N_EXPERTS, D_EXPERT, dm), D_EXPERT ** -0.5),
    }


def reference(x, c, ctx, c_ctx, ada_w, ada_b, norm1_g, norm2_g, w_in, w_out,
              rg_conv_w, rg_conv_b, rg_gate_w, rg_gate_b, rg_lambda,
              na_qk_g, na_rpb, da_qk_g, da_lambda, da_subln_g,
              router_w, router_b, moe_w_gate, moe_w_up, moe_w_down):
    bsz, seq, dm = x.shape
    rope = rope_2d_tables(seq)
    cs = ctx
    act_c = jax.nn.silu(c)
    act_cc = jax.nn.silu(c_ctx)
    n_lat = bsz * seq
    for l in range(DEPTH):
        need_ctx = l < DEPTH - 1
        lambda_init = 0.8 - 0.6 * math.exp(-0.3 * l)
        mx = jnp.split((act_c @ ada_w[l] + ada_b[l])[:, None, :], 6, axis=-1)
        mc = jnp.split(act_cc @ ada_w[l] + ada_b[l], 6, axis=-1)

        hx = modulate(rms_norm(x, norm1_g[l]), mx[0], mx[1])
        hc = modulate(rms_norm(cs, norm1_g[l]), mc[0], mc[1])
        ox, oc = token_mixers(hx, hc, w_in[l], rg_conv_w[l], rg_conv_b[l], rg_gate_w[l], rg_gate_b[l],
                              rg_lambda[l], na_qk_g[l], na_rpb[l], da_qk_g[l], da_lambda[l],
                              da_subln_g[l], lambda_init, rope, need_ctx)
        x = x + mx[2] * (ox @ w_out[l])

        tokens = modulate(rms_norm(x, norm2_g[l]), mx[3], mx[4]).reshape(-1, dm)
        if need_ctx:
            cs = cs + mc[2] * (oc @ w_out[l])
            hc2 = modulate(rms_norm(cs, norm2_g[l]), mc[3], mc[4])
            tokens = jnp.concatenate([tokens, hc2.reshape(-1, dm)], axis=0)
        eidx, gw = route_grouped(tokens, router_w, router_b)
        y = moe_experts(tokens, eidx, gw, moe_w_gate[l], moe_w_up[l], moe_w_down[l])
        x = x + mx[5] * y[:n_lat].reshape(bsz, seq, dm)
        if need_ctx:
            cs = cs + mc[5] * y[n_lat:].reshape(cs.shape)
    return x
```

```python
import functools
import math

import numpy as np
import jax
import jax.numpy as jnp
from jax import lax
from jax.experimental import pallas as pl
from jax.experimental.pallas import tpu as pltpu

F32 = jnp.float32
BF16 = jnp.bfloat16

EPS = 1e-6
GRID_W = 64
RG_BLOCKS = 8
RG_C = 8.0
NA_HEAD_DIM = 128
NA_WIN_H = 8
NA_WIN_W = 16
DA_HEAD_DIM = 64
DA_V_DIM = 128
ROPE_THETA = 10000.0
N_EXPERTS = 16
N_GROUPS = 4
GROUP_SIZE = 4

LANES = 128
VMEM_LIMIT_BYTES = 56 * 1024 * 1024
SEQ_BLOCK = 256
MOE_ROWS = 512
MOE_CHUNK = 256
MOE_FTILE = 256
ROW_CHUNKS = 16
NORM_CHUNK = 128
MM_CHUNK = 256
NEG_BIG = -1e30


def _cparams(sem):
    return pltpu.CompilerParams(dimension_semantics=sem, vmem_limit_bytes=VMEM_LIMIT_BYTES)


def _dot(a, b):
    return jnp.dot(a, b, preferred_element_type=F32)


def _dot_t(a, b):
    return lax.dot_general(a, b, (((1,), (1,)), ((), ())), preferred_element_type=F32)


def _row_tile(n_rows, cap, step=MM_CHUNK):
    best = None
    for t in range(step, cap + 1, step):
        if n_rows % t == 0:
            best = t
    assert best is not None, n_rows
    return best


def _sigmoid(z):
    return 1.0 / (1.0 + jnp.exp(-z))


def _row_loop(n_rows, chunk, body):
    assert n_rows % chunk == 0

    def step(i, carry):
        body(pl.multiple_of(i * chunk, chunk))
        return carry

    lax.fori_loop(0, n_rows // chunk, step, 0)


def _ada_kernel(a_ref, w_ref, b_ref, o_ref):
    chunk = 256
    tn = w_ref.shape[1]

    def step(i, carry):
        r0, r1 = carry
        k0 = pl.multiple_of(i * chunk, chunk)
        a = a_ref[pl.ds(k0, chunk), :]
        act = a * _sigmoid(a)
        w = w_ref[pl.ds(k0, chunk), :]
        r0 = r0 + jnp.sum(w * act[:, 0:1], axis=0, keepdims=True)
        r1 = r1 + jnp.sum(w * act[:, 1:2], axis=0, keepdims=True)
        return r0, r1

    r0, r1 = lax.fori_loop(0, w_ref.shape[0] // chunk, step, (b_ref[...], b_ref[...]))
    o_ref[...] = jnp.concatenate([r0, r1, jnp.zeros((6, tn), F32)], axis=0)


def _ada(cvec_t, ada_w, ada_b):
    depth, dm, n6 = ada_w.shape
    tn = 1024
    return pl.pallas_call(
        _ada_kernel,
        grid=(depth, n6 // tn),
        in_specs=[
            pl.BlockSpec((dm, 2), lambda l, j: (0, 0)),
            pl.BlockSpec((None, dm, tn), lambda l, j: (l, 0, j)),
            pl.BlockSpec((None, 1, tn), lambda l, j: (l, 0, j)),
        ],
        out_specs=pl.BlockSpec((None, 8, tn), lambda l, j: (l, 0, j)),
        out_shape=jax.ShapeDtypeStruct((depth, 8, n6), F32),
        compiler_params=_cparams(("parallel", "parallel")),
        name="ada_mod",
    )(cvec_t, ada_w, ada_b.reshape(depth, 1, n6))


def _mod_rows(mod_ref, col0, width, is_ctx):
    return jnp.where(is_ctx, mod_ref[1:2, col0:col0 + width], mod_ref[0:1, col0:col0 + width])


def _norm_modulate(x, g, mod_ref, shift_col, scale_col, row0, n_lat, dm):
    y = x * lax.rsqrt(jnp.mean(x * x, axis=-1, keepdims=True) + EPS) * g
    row = row0 + lax.broadcasted_iota(jnp.int32, (x.shape[0], 1), 0)
    is_ctx = row >= n_lat
    return y * (1.0 + _mod_rows(mod_ref, scale_col, dm, is_ctx)) + _mod_rows(mod_ref, shift_col, dm, is_ctx)


def _inproj_kernel(x_ref, g_ref, mod_ref, w_ref, o_ref, h_ref, *, n_lat, tm, dm):
    i = pl.program_id(0)

    @pl.when(pl.program_id(1) == 0)
    def _():
        def norm_rows(r0):
            rs = pl.ds(r0, NORM_CHUNK)
            h_ref[rs, :] = _norm_modulate(x_ref[rs, :], g_ref[...], mod_ref, 0, dm, i * tm + r0,
                                          n_lat, dm).astype(BF16)
        _row_loop(tm, NORM_CHUNK, norm_rows)

    def mm_rows(r0):
        rs = pl.ds(r0, MM_CHUNK)
        o_ref[rs, :] = _dot(h_ref[rs, :], w_ref[...]).astype(BF16)
    _row_loop(tm, MM_CHUNK, mm_rows)


def _inproj(xs, g, mod_l, w_b, n_lat):
    n_rows, dm = xs.shape
    n_out = w_b.shape[1]
    tm = _row_tile(n_rows, 1280)
    tn = 1024
    return pl.pallas_call(
        functools.partial(_inproj_kernel, n_lat=n_lat, tm=tm, dm=dm),
        grid=(n_rows // tm, n_out // tn),
        in_specs=[
            pl.BlockSpec((tm, dm), lambda i, j: (i, 0)),
            pl.BlockSpec((1, dm), lambda i, j: (0, 0)),
            pl.BlockSpec(mod_l.shape, lambda i, j: (0, 0)),
            pl.BlockSpec((dm, tn), lambda i, j: (0, j)),
        ],
        out_specs=pl.BlockSpec((tm, tn), lambda i, j: (i, j)),
        out_shape=jax.ShapeDtypeStruct((n_rows, n_out), BF16),
        scratch_shapes=[pltpu.VMEM((tm, dm), BF16)],
        compiler_params=_cparams(("parallel", "arbitrary")),
        name="in_proj",
    )(xs, g.reshape(1, dm), mod_l, w_b)


def _gelu_tanh(g):
    return 0.5 * g * (1.0 + jnp.tanh(0.7978845608028654 * (g + 0.044715 * g * g * g)))


def _rg_kernel(*refs, reverse, n_blk_lat, t_blk, width):
    if reverse:
        (xc_ref, xp_ref, xn_ref, cw_ref, cb_ref, gw_ref, gb_ref, lam_ref,
         hf_ref, gate_ref, o_ref, carry_ref) = refs
    else:
        (xc_ref, xp_ref, xn_ref, cw_ref, cb_ref, gw_ref, gb_ref, lam_ref,
         o_ref, carry_ref) = refs
    s = pl.program_id(0)
    if reverse:
        blk = jnp.where(s == 0, n_blk_lat, n_blk_lat - s)
    else:
        blk = jnp.where(s == 0, n_blk_lat, s - 1)
    seq_first = jnp.logical_or(blk == 0, blk == n_blk_lat)
    seq_last = jnp.logical_or(blk == n_blk_lat - 1, blk == n_blk_lat)
    @pl.when(s == 0)
    def _():
        carry_ref[...] = jnp.zeros_like(carry_ref)

    row = lax.broadcasted_iota(jnp.int32, (t_blk, LANES), 0)
    for n in range(width // LANES):
        cs = slice(n * LANES, (n + 1) * LANES)
        x = xc_ref[:, cs].astype(F32)
        xp = jnp.where(seq_first, 0.0, xp_ref[:, cs].astype(F32))
        xn = jnp.where(seq_last, 0.0, xn_ref[:, cs].astype(F32))
        xe = jnp.concatenate([xp, x, xn], axis=0)
        cw = cw_ref[:, cs]
        u = (cw[0:1] * xe[6:6 + t_blk] + cw[1:2] * xe[7:7 + t_blk]
             + cw[2:3] * xe[8:8 + t_blk] + cw[3:4] * xe[9:9 + t_blk] + cb_ref[:, cs])
        z = _dot(u.astype(BF16), gw_ref[n]) + gb_ref[n]
        r = _sigmoid(z[:, :LANES])
        gi = _sigmoid(z[:, LANES:])
        zl = -lam_ref[:, cs]
        softplus = jnp.maximum(zl, 0.0) + jnp.log(1.0 + jnp.exp(-jnp.abs(zl)))
        a = jnp.exp(-RG_C * r * softplus)
        b = jnp.sqrt(1.0 - a * a) * (gi * u)
        d = 1
        while d < t_blk:
            if reverse:
                a_s = pltpu.roll(a, t_blk - d, 0)
                b_s = pltpu.roll(b, t_blk - d, 0)
                m = row < t_blk - d
            else:
                a_s = pltpu.roll(a, d, 0)
                b_s = pltpu.roll(b, d, 0)
                m = row >= d
            b = jnp.where(m, a * b_s + b, b)
            a = jnp.where(m, a * a_s, a)
            d *= 2
        h = a * carry_ref[:, cs] + b
        carry_ref[:, cs] = h[0:1] if reverse else h[t_blk - 1:t_blk]
        if reverse:
            o_ref[:, cs] = ((hf_ref[:, cs] + h) * _gelu_tanh(gate_ref[:, cs].astype(F32))).astype(BF16)
        else:
            o_ref[:, cs] = h


def _rg_direction(p, cw, cb, gw, gb, lam, hf, *, reverse, n_lat, width):
    n_rows = p.shape[0]
    t_blk = SEQ_BLOCK
    n_blk_lat = n_lat // t_blk
    n_steps = n_rows // t_blk
    halo_blocks = n_rows // 8
    per = t_blk // 8

    if reverse:
        def blk_of(s):
            return jnp.where(s == 0, n_blk_lat, n_blk_lat - s)
    else:
        def blk_of(s):
            return jnp.where(s == 0, n_blk_lat, s - 1)

    in_specs = [
        pl.BlockSpec((t_blk, width), lambda s: (blk_of(s), 0)),
        pl.BlockSpec((8, width), lambda s: (jnp.maximum(blk_of(s) * per - 1, 0), 0)),
        pl.BlockSpec((8, width), lambda s: (jnp.minimum((blk_of(s) + 1) * per, halo_blocks - 1), 0)),
        pl.BlockSpec((4, width), lambda s: (0, 0)),
        pl.BlockSpec((1, width), lambda s: (0, 0)),
        pl.BlockSpec((RG_BLOCKS, LANES, 2 * LANES), lambda s: (0, 0, 0)),
        pl.BlockSpec((RG_BLOCKS, 1, 2 * LANES), lambda s: (0, 0, 0)),
        pl.BlockSpec((1, width), lambda s: (0, 0)),
    ]
    args = [p, p, p, cw, cb, gw, gb, lam]
    if reverse:
        in_specs += [
            pl.BlockSpec((t_blk, width), lambda s: (blk_of(s), 0)),
            pl.BlockSpec((t_blk, width), lambda s: (blk_of(s), 1)),
        ]
        args += [hf, p]
        out_dtype = BF16
    else:
        out_dtype = F32
    return pl.pallas_call(
        functools.partial(_rg_kernel, reverse=reverse, n_blk_lat=n_blk_lat, t_blk=t_blk, width=width),
        grid=(n_steps,),
        in_specs=in_specs,
        out_specs=pl.BlockSpec((t_blk, width), lambda s: (blk_of(s), 0)),
        out_shape=jax.ShapeDtypeStruct((n_rows, width), out_dtype),
        scratch_shapes=[pltpu.VMEM((1, width), F32)],
        compiler_params=_cparams(("arbitrary",)),
        name="rg_lru_bwd" if reverse else "rg_lru_fwd",
    )(*args)


def _rg_group(p, conv_w, conv_b, gate_w, gate_b, lam, n_lat):
    width = conv_w.shape[1]
    outs = None
    for d in range(2):
        gw = jnp.concatenate([gate_w[d, 0], gate_w[d, 1]], axis=-1).astype(BF16)
        gb = jnp.concatenate([gate_b[d, 0].reshape(RG_BLOCKS, 1, LANES),
                              gate_b[d, 1].reshape(RG_BLOCKS, 1, LANES)], axis=-1)
        outs = _rg_direction(p, conv_w, conv_b.reshape(1, width), gw, gb, lam[d].reshape(1, width),
                             outs, reverse=(d == 1), n_lat=n_lat, width=width)
    return outs


def _group_norm(x, bd, gain, group):
    sq = x * x
    hi = sq.astype(BF16)
    lo = (sq - hi.astype(F32)).astype(BF16)
    ss = _dot(hi, bd) + _dot(lo, bd)
    return x * lax.rsqrt(ss * (1.0 / group) + EPS) * gain


def _qkprep_kernel(nq_ref, nk_ref, dq_ref, dk_ref, gains_ref, bd128_ref, bd64_ref,
                   cos_ref, sa_ref, sb_ref, onq_ref, onk_ref, odq_ref, odk_ref,
                   *, na_scale, da_scale):
    tm, w = nq_ref.shape
    reps = w // LANES

    def rows(r0):
        rs = pl.ds(r0, NORM_CHUNK)
        gains = gains_ref[...]
        onq_ref[rs, :] = (_group_norm(nq_ref[rs, :].astype(F32), bd128_ref[...], gains[0:1], NA_HEAD_DIM)
                          * na_scale).astype(BF16)
        onk_ref[rs, :] = _group_norm(nk_ref[rs, :].astype(F32), bd128_ref[...], gains[1:2],
                                     NA_HEAD_DIM).astype(BF16)
        cos = jnp.concatenate([cos_ref[rs, :]] * reps, axis=1)
        sa = jnp.concatenate([sa_ref[rs, :]] * reps, axis=1)
        sb = jnp.concatenate([sb_ref[rs, :]] * reps, axis=1)

        def rope(y):
            return y * cos + pltpu.roll(y, 16, 1) * sa + pltpu.roll(y, w - 16, 1) * sb

        q = rope(_group_norm(dq_ref[rs, :].astype(F32), bd64_ref[...], gains[2:3], DA_HEAD_DIM)) * da_scale
        k = rope(_group_norm(dk_ref[rs, :].astype(F32), bd64_ref[...], gains[3:4], DA_HEAD_DIM))
        lane = lax.broadcasted_iota(jnp.int32, q.shape, 1)
        first_map = (lane % LANES) < DA_HEAD_DIM
        odq_ref[0, rs, :] = jnp.where(first_map, q, 0.0).astype(BF16)
        odq_ref[1, rs, :] = jnp.where(first_map, 0.0, q).astype(BF16)
        odk_ref[rs, :] = k.astype(BF16)

    _row_loop(tm, NORM_CHUNK, rows)


def _qkprep(p, na_g, da_g, rope_tabs, col_blocks, w):
    n_rows = p.shape[0]
    tm = _row_tile(n_rows, 1280)
    gains = jnp.stack([jnp.tile(na_g[0], w // NA_HEAD_DIM), jnp.tile(na_g[1], w // NA_HEAD_DIM),
                       jnp.tile(da_g[0], w // DA_HEAD_DIM), jnp.tile(da_g[1], w // DA_HEAD_DIM)])
    lane = np.arange(w)
    bd128 = jnp.asarray((lane[:, None] // NA_HEAD_DIM) == (lane[None, :] // NA_HEAD_DIM), BF16)
    bd64 = jnp.asarray((lane[:, None] // DA_HEAD_DIM) == (lane[None, :] // DA_HEAD_DIM), BF16)
    cos, sa, sb = rope_tabs
    row_blk = lambda c: pl.BlockSpec((tm, w), lambda i, c=c: (i, c))
    const = lambda shape: pl.BlockSpec(shape, lambda i: (0,) * len(shape))
    tab = pl.BlockSpec((tm, LANES), lambda i: (i, 0))
    return pl.pallas_call(
        functools.partial(_qkprep_kernel, na_scale=NA_HEAD_DIM ** -0.5, da_scale=DA_HEAD_DIM ** -0.5),
        grid=(n_rows // tm,),
        in_specs=[row_blk(col_blocks[0]), row_blk(col_blocks[1]), row_blk(col_blocks[2]),
                  row_blk(col_blocks[3]), const((4, w)), const((w, w)), const((w, w)), tab, tab, tab],
        out_specs=[pl.BlockSpec((tm, w), lambda i: (i, 0)), pl.BlockSpec((tm, w), lambda i: (i, 0)),
                   pl.BlockSpec((2, tm, w), lambda i: (0, i, 0)), pl.BlockSpec((tm, w), lambda i: (i, 0))],
        out_shape=[jax.ShapeDtypeStruct((n_rows, w), BF16), jax.ShapeDtypeStruct((n_rows, w), BF16),
                   jax.ShapeDtypeStruct((2, n_rows, w), BF16), jax.ShapeDtypeStruct((n_rows, w), BF16)],
        compiler_params=_cparams(("parallel",)),
        name="qk_prep",
    )(p, p, p, p, gains, bd128, bd64, cos, sa, sb)


def _rope_tables(n_lat, n_ctx):
    t = jnp.arange(n_lat, dtype=jnp.int32)
    n = DA_HEAD_DIM // 4
    inv = 1.0 / (ROPE_THETA ** (jnp.arange(n, dtype=F32) / n))
    ang_r = (t // GRID_W).astype(F32)[:, None] * inv
    ang_c = (t % GRID_W).astype(F32)[:, None] * inv
    cr, sr, cc, sc = jnp.cos(ang_r), jnp.sin(ang_r), jnp.cos(ang_c), jnp.sin(ang_c)
    z = jnp.zeros_like(sr)
    cos = jnp.concatenate([cr, cr, cc, cc], axis=1)
    sa = jnp.concatenate([z, sr, z, sc], axis=1)
    sb = jnp.concatenate([-sr, z, -sc, z], axis=1)

    def full(tab, ctx_val):
        tab = jnp.concatenate([tab, tab], axis=1)
        return jnp.concatenate([tab, jnp.full((n_ctx, LANES), ctx_val, F32)], axis=0)

    return full(cos, 1.0), full(sa, 0.0), full(sb, 0.0)


NA_QROWS = SEQ_BLOCK // GRID_W


def _na_bias(rpb, rows):
    n_blk = rows // NA_QROWS
    n_key = 3 * NA_QROWS
    idx_r = np.zeros((3, NA_QROWS, n_key), np.int32)
    val_r = np.zeros((3, NA_QROWS, n_key), bool)
    for v, b in enumerate((0, min(1, n_blk - 1), n_blk - 1)):
        for i in range(NA_QROWS):
            r = NA_QROWS * b + i
            rs = min(max(r - NA_WIN_H // 2, 0), rows - NA_WIN_H)
            for j in range(n_key):
                kr = NA_QROWS * (b - 1) + j
                val_r[v, i, j] = rs <= kr < rs + NA_WIN_H
                idx_r[v, i, j] = min(max(kr - r + NA_WIN_H - 1, 0), 2 * NA_WIN_H - 2)
    col = np.arange(GRID_W)
    cstart = np.clip(col - NA_WIN_W // 2, 0, GRID_W - NA_WIN_W)
    val_c = (col[None, :] >= cstart[:, None]) & (col[None, :] < cstart[:, None] + NA_WIN_W)
    idx_c = np.clip(col[None, :] - col[:, None] + NA_WIN_W - 1, 0, 2 * NA_WIN_W - 2)
    heads = rpb.shape[0]
    bcol = jnp.where(val_c[None, None], rpb[:, :, idx_c], NEG_BIG)
    big = bcol[:, idx_r]
    big = jnp.where(val_r[None, :, :, :, None, None], big, NEG_BIG)
    big = jnp.transpose(big, (1, 0, 2, 4, 3, 5))
    return big.reshape(3, heads, NA_QROWS * GRID_W, n_key * GRID_W).astype(F32)


def _na_kernel(q_ref, kp_ref, kc_ref, kn_ref, kx_ref, vp_ref, vc_ref, vn_ref, vx_ref, bias_ref,
               o_ref, *, n_blk_lat):
    b = pl.program_id(1)
    t = SEQ_BLOCK
    q = q_ref[...]
    sx = _dot_t(q, kx_ref[...])

    @pl.when(b < n_blk_lat)
    def _():
        s0 = _dot_t(q, kp_ref[...]) + bias_ref[:, 0:t]
        s1 = _dot_t(q, kc_ref[...]) + bias_ref[:, t:2 * t]
        s2 = _dot_t(q, kn_ref[...]) + bias_ref[:, 2 * t:3 * t]
        m = jnp.maximum(jnp.maximum(jnp.max(s0, -1, keepdims=True), jnp.max(s1, -1, keepdims=True)),
                        jnp.maximum(jnp.max(s2, -1, keepdims=True), jnp.max(sx, -1, keepdims=True)))
        p0, p1, p2, px = jnp.exp(s0 - m), jnp.exp(s1 - m), jnp.exp(s2 - m), jnp.exp(sx - m)
        l = (jnp.sum(p0, -1, keepdims=True) + jnp.sum(p1, -1, keepdims=True)
             + jnp.sum(p2, -1, keepdims=True) + jnp.sum(px, -1, keepdims=True))
        o = (_dot(p0.astype(BF16), vp_ref[...]) + _dot(p1.astype(BF16), vc_ref[...])
             + _dot(p2.astype(BF16), vn_ref[...]) + _dot(px.astype(BF16), vx_ref[...]))
        o_ref[...] = (o / l).astype(BF16)

    @pl.when(b == n_blk_lat)
    def _():
        m = jnp.max(sx, -1, keepdims=True)
        px = jnp.exp(sx - m)
        o = _dot(px.astype(BF16), vx_ref[...])
        o_ref[...] = (o / jnp.sum(px, -1, keepdims=True)).astype(BF16)


def _na_attention(nq, nk, p, bias, v_col0, n_lat, heads):
    n_rows = nq.shape[0]
    t = SEQ_BLOCK
    n_blk_lat = n_lat // t
    n_blk = n_rows // t
    last = n_blk_lat - 1

    def q_map(h, b):
        return (b, h)

    def prev_map(h, b):
        return (jnp.clip(b - 1, 0, last), h)

    def cur_map(h, b):
        return (jnp.minimum(b, last), h)

    def next_map(h, b):
        return (jnp.clip(b + 1, 0, last), h)

    def ctx_map(h, b):
        return (n_blk_lat, h)

    def voff(fn):
        return lambda h, b: (fn(h, b)[0], fn(h, b)[1] + v_col0)

    def var_map(h, b):
        return (jnp.where(b == 0, 0, jnp.where(b >= last, 2, 1)), h, 0, 0)

    blk = lambda fn: pl.BlockSpec((t, NA_HEAD_DIM), fn)
    return pl.pallas_call(
        functools.partial(_na_kernel, n_blk_lat=n_blk_lat),
        grid=(heads, n_blk),
        in_specs=[blk(q_map), blk(prev_map), blk(cur_map), blk(next_map), blk(ctx_map),
                  blk(voff(prev_map)), blk(voff(cur_map)), blk(voff(next_map)), blk(voff(ctx_map)),
                  pl.BlockSpec((None, None, t, 3 * t), var_map)],
        out_specs=pl.BlockSpec((t, NA_HEAD_DIM), q_map),
        out_shape=jax.ShapeDtypeStruct((n_rows, heads * NA_HEAD_DIM), BF16),
        compiler_params=_cparams(("parallel", "arbitrary")),
        name="na_attn",
    )(nq, nk, nk, nk, nk, p, p, p, p, bias)


def _da_kernel(q_ref, k_ref, v_ref, dl_ref, sg_ref, o_ref, m_ref, l_ref, acc_ref,
               *, n_lat, n_ctx, tk, lambda_init):
    qb = pl.program_id(1)
    tq = q_ref.shape[1]
    q = q_ref[...].reshape(2 * tq, q_ref.shape[2])
    m_ref[...] = jnp.full(m_ref.shape, -jnp.inf, F32)
    l_ref[...] = jnp.zeros_like(l_ref)
    acc_ref[...] = jnp.zeros_like(acc_ref)

    def chunk(start, size):
        k = k_ref[pl.ds(start, size), :]
        v = v_ref[pl.ds(start, size), :]
        s = _dot_t(q, k)
        m_old = m_ref[...]
        m_new = jnp.maximum(m_old, jnp.max(s, -1, keepdims=True))
        alpha = jnp.exp(m_old - m_new)
        p = jnp.exp(s - m_new)
        l_ref[...] = alpha * l_ref[...] + jnp.sum(p, -1, keepdims=True)
        acc_ref[...] = alpha * acc_ref[...] + _dot(p.astype(BF16), v)
        m_ref[...] = m_new

    @pl.when(qb < n_lat // tq)
    def _():
        def body(c, carry):
            chunk(pl.multiple_of(c * tk, tk), tk)
            return carry
        lax.fori_loop(0, n_lat // tk, body, 0)

    chunk(n_lat, n_ctx)

    o = acc_ref[...] / l_ref[...]
    dl = dl_ref[...]
    lam = (jnp.exp(jnp.sum(dl[0:1] * dl[1:2], keepdims=True))
           - jnp.exp(jnp.sum(dl[2:3] * dl[3:4], keepdims=True)) + lambda_init)
    d = o[:tq] - lam * o[tq:]
    y = d * lax.rsqrt(jnp.mean(d * d, axis=-1, keepdims=True) + EPS) * sg_ref[...]
    o_ref[...] = (y * (1.0 - lambda_init)).astype(BF16)


def _da_attention(dqq, dk, p, da_lam, subln_g, v_col0, n_lat, heads, lambda_init):
    n_rows = dk.shape[0]
    n_ctx = n_rows - n_lat
    tq = SEQ_BLOCK
    tk = 512 if n_lat % 512 == 0 else SEQ_BLOCK
    return pl.pallas_call(
        functools.partial(_da_kernel, n_lat=n_lat, n_ctx=n_ctx, tk=tk, lambda_init=lambda_init),
        grid=(heads, n_rows // tq),
        in_specs=[
            pl.BlockSpec((2, tq, DA_V_DIM), lambda h, b: (0, b, h)),
            pl.BlockSpec((n_rows, DA_V_DIM), lambda h, b: (0, h)),
            pl.BlockSpec((n_rows, DA_V_DIM), lambda h, b: (0, h + v_col0)),
            pl.BlockSpec((4, DA_HEAD_DIM), lambda h, b: (0, 0)),
            pl.BlockSpec((1, DA_V_DIM), lambda h, b: (0, 0)),
        ],
        out_specs=pl.BlockSpec((tq, DA_V_DIM), lambda h, b: (b, h)),
        out_shape=jax.ShapeDtypeStruct((n_rows, heads * DA_V_DIM), BF16),
        scratch_shapes=[pltpu.VMEM((2 * tq, 1), F32), pltpu.VMEM((2 * tq, 1), F32),
                        pltpu.VMEM((2 * tq, DA_V_DIM), F32)],
        compiler_params=_cparams(("parallel", "arbitrary")),
        name="da_attn",
    )(dqq, dk, p, da_lam, subln_g.reshape(1, DA_V_DIM))


def _outproj_kernel(ra_ref, nb_ref, da_ref, w_ref, x_ref, gate_ref, o_ref, *, n_lat, tm, k_ra, k_nb):
    i = pl.program_id(0)

    def rows(r0):
        rs = pl.ds(r0, MM_CHUNK)
        acc = (_dot(ra_ref[rs, :], w_ref[0:k_ra, :]) + _dot(nb_ref[rs, :], w_ref[k_ra:k_ra + k_nb, :])
               + _dot(da_ref[rs, :], w_ref[k_ra + k_nb:, :]))
        row = i * tm + r0 + lax.broadcasted_iota(jnp.int32, (MM_CHUNK, 1), 0)
        gate = jnp.where(row >= n_lat, gate_ref[1:2, :], gate_ref[0:1, :])
        o_ref[rs, :] = x_ref[rs, :] + gate * acc

    _row_loop(tm, MM_CHUNK, rows)


def _outproj(ra, nb, da, w_b, xs, mod_l, n_lat):
    n_rows, dm = xs.shape
    tm = _row_tile(n_rows, 1280)
    tn = 1024
    gate_blk0 = 2 * dm // tn
    return pl.pallas_call(
        functools.partial(_outproj_kernel, n_lat=n_lat, tm=tm, k_ra=ra.shape[1], k_nb=nb.shape[1]),
        grid=(n_rows // tm, dm // tn),
        in_specs=[
            pl.BlockSpec((tm, ra.shape[1]), lambda i, j: (i, 0)),
            pl.BlockSpec((tm, nb.shape[1]), lambda i, j: (i, 0)),
            pl.BlockSpec((tm, da.shape[1]), lambda i, j: (i, 0)),
            pl.BlockSpec((w_b.shape[0], tn), lambda i, j: (0, j)),
            pl.BlockSpec((tm, tn), lambda i, j: (i, j)),
            pl.BlockSpec((8, tn), lambda i, j: (0, gate_blk0 + j)),
        ],
        out_specs=pl.BlockSpec((tm, tn), lambda i, j: (i, j)),
        out_shape=jax.ShapeDtypeStruct((n_rows, dm), F32),
        compiler_params=_cparams(("parallel", "parallel")),
        name="out_proj",
    )(ra, nb, da, w_b, xs, mod_l)


def _split_bf16(v):
    hi = v.astype(BF16)
    return hi, (v - hi.astype(F32)).astype(BF16)


def _router_kernel(x_ref, g_ref, mod_ref, rw_ref, rb_ref, tri_ref,
                   tok_ref, eidx_ref, wgt_ref, rank_ref, cnt_ref, run_ref, logit_ref, *, n_lat, tm, dm):
    i = pl.program_id(0)

    @pl.when(i == 0)
    def _():
        run_ref[...] = jnp.zeros_like(run_ref)

    n_chunks = dm // LANES

    def rows(r0):
        rs = pl.ds(r0, NORM_CHUNK)
        h = _norm_modulate(x_ref[rs, :], g_ref[...], mod_ref, 3 * dm, 4 * dm, i * tm + r0, n_lat, dm)
        for c in range(n_chunks):
            tok_ref[pl.ds(r0 * n_chunks + c, NORM_CHUNK, stride=n_chunks), :] = h[:, c * LANES:(c + 1) * LANES]
        h_hi, h_lo = _split_bf16(h)
        w_hi, w_lo = _split_bf16(rw_ref[...])
        logit_ref[:, rs] = _dot_t(w_hi, h_hi) + _dot_t(w_hi, h_lo) + _dot_t(w_lo, h_hi)

    _row_loop(tm, NORM_CHUNK, rows)
    aff = _sigmoid(logit_ref[...])
    sel = aff + rb_ref[...]

    def rows(a, e):
        return a[e:e + 1, :]

    best_g = jnp.zeros((1, tm), jnp.int32)
    best_s = None
    for g in range(N_GROUPS):
        v = [rows(sel, g * GROUP_SIZE + k) for k in range(GROUP_SIZE)]
        sc = None
        for a in range(GROUP_SIZE):
            for b in range(a + 1, GROUP_SIZE):
                pair = v[a] + v[b]
                sc = pair if sc is None else jnp.maximum(sc, pair)
        if best_s is None:
            best_s = sc
        else:
            upd = sc > best_s
            best_g = jnp.where(upd, g, best_g)
            best_s = jnp.where(upd, sc, best_s)

    def pick(a, k):
        out = rows(a, k)
        for g in range(1, N_GROUPS):
            out = jnp.where(best_g == g, rows(a, g * GROUP_SIZE + k), out)
        return out

    cand = [pick(sel, k) for k in range(GROUP_SIZE)]
    cand_aff = [pick(aff, k) for k in range(GROUP_SIZE)]
    i1 = jnp.zeros((1, tm), jnp.int32)
    v1 = cand[0]
    for k in range(1, GROUP_SIZE):
        upd = cand[k] > v1
        i1 = jnp.where(upd, k, i1)
        v1 = jnp.where(upd, cand[k], v1)
    i2 = jnp.full((1, tm), -1, jnp.int32)
    v2 = jnp.full((1, tm), -jnp.inf, F32)
    for k in range(GROUP_SIZE):
        upd = jnp.logical_and(i1 != k, cand[k] > v2)
        i2 = jnp.where(upd, k, i2)
        v2 = jnp.where(upd, cand[k], v2)
    a1 = cand_aff[0]
    a2 = cand_aff[0]
    for k in range(1, GROUP_SIZE):
        a1 = jnp.where(i1 == k, cand_aff[k], a1)
        a2 = jnp.where(i2 == k, cand_aff[k], a2)
    e1 = best_g * GROUP_SIZE + i1
    e2 = best_g * GROUP_SIZE + i2
    den = a1 + a2
    eidx_ref[...] = jnp.concatenate([e1, e2], axis=0)
    wgt_ref[...] = jnp.concatenate([a1 / den, a2 / den], axis=0)

    eid = lax.broadcasted_iota(jnp.int32, (N_EXPERTS, tm), 0)
    member = jnp.logical_or(eid == e1, eid == e2)
    memf = jnp.where(member, 1.0, 0.0)
    before = _dot(memf.astype(BF16), tri_ref[...]) + run_ref[:, 0:1]
    r1 = jnp.sum(jnp.where(eid == e1, before, 0.0), axis=0, keepdims=True)
    r2 = jnp.sum(jnp.where(eid == e2, before, 0.0), axis=0, keepdims=True)
    rank_ref[...] = jnp.concatenate([r1, r2], axis=0).astype(jnp.int32)
    run_ref[...] = run_ref[...] + jnp.sum(memf, axis=1, keepdims=True)
    cnt_ref[...] = run_ref[...].astype(jnp.int32)


def _router(xs, g, mod_l, router_w, router_b, n_lat):
    n_rows, dm = xs.shape
    tm = _row_tile(n_rows, 1280)
    n_e = router_w.shape[1]
    tri = jnp.asarray(np.arange(tm)[:, None] < np.arange(tm)[None, :], BF16)
    const = lambda shape: pl.BlockSpec(shape, lambda i: (0,) * len(shape))
    pair = pl.BlockSpec((2, tm), lambda i: (0, i))
    return pl.pallas_call(
        functools.partial(_router_kernel, n_lat=n_lat, tm=tm, dm=dm),
        grid=(n_rows // tm,),
        in_specs=[pl.BlockSpec((tm, dm), lambda i: (i, 0)), const((1, dm)), const(mod_l.shape),
                  const((n_e, dm)), const((n_e, 1)), const((tm, tm))],
        out_specs=[pl.BlockSpec((tm * ROW_CHUNKS, LANES), lambda i: (i, 0)), pair, pair, pair,
                   const((n_e, LANES))],
        out_shape=[jax.ShapeDtypeStruct((n_rows * ROW_CHUNKS, LANES), F32),
                   jax.ShapeDtypeStruct((2, n_rows), jnp.int32),
                   jax.ShapeDtypeStruct((2, n_rows), F32),
                   jax.ShapeDtypeStruct((2, n_rows), jnp.int32),
                   jax.ShapeDtypeStruct((n_e, LANES), jnp.int32)],
        scratch_shapes=[pltpu.VMEM((n_e, LANES), F32), pltpu.VMEM((n_e, tm), F32)],
        compiler_params=_cparams(("arbitrary",)),
        name="router",
    )(xs, g.reshape(1, dm), mod_l, router_w.T, router_b.reshape(n_e, 1), tri)


def _dispatch_kernel(pos_ref, tok_ref, buf_in_ref, buf_ref, sem, *, tb):
    del buf_in_ref
    i = pl.program_id(0)

    def row_copy(t, k):
        src = tok_ref.at[pl.ds(pl.multiple_of((i * tb + t) * ROW_CHUNKS, ROW_CHUNKS), ROW_CHUNKS), :]
        dst = buf_ref.at[pl.ds(pl.multiple_of(pos_ref[k, t], ROW_CHUNKS), ROW_CHUNKS), :]
        return pltpu.make_async_copy(src, dst, sem)

    def issue(t, carry):
        row_copy(t, 0).start()
        row_copy(t, 1).start()
        return carry

    def drain(t, carry):
        row_copy(t, 0).wait()
        row_copy(t, 1).wait()
        return carry

    lax.fori_loop(0, tb, issue, 0)
    lax.fori_loop(0, tb, drain, 0)


def _dispatch(tok, pos16, n_slots):
    n_rows = pos16.shape[1]
    tb = _row_tile(n_rows, 1280)
    buf0 = jnp.zeros((n_slots * ROW_CHUNKS, LANES), F32)
    return pl.pallas_call(
        functools.partial(_dispatch_kernel, tb=tb),
        grid=(n_rows // tb,),
        in_specs=[pl.BlockSpec((2, tb), lambda i: (0, i), memory_space=pltpu.SMEM),
                  pl.BlockSpec(memory_space=pl.ANY), pl.BlockSpec(memory_space=pl.ANY)],
        out_specs=pl.BlockSpec(memory_space=pl.ANY),
        out_shape=jax.ShapeDtypeStruct(buf0.shape, F32),
        scratch_shapes=[pltpu.SemaphoreType.DMA(())],
        input_output_aliases={2: 0},
        compiler_params=_cparams(("arbitrary",)),
        name="moe_dispatch",
    )(pos16, tok, buf0)


def _expert_kernel(e_ref, r_ref, x_ref, wg_ref, wu_ref, wd_ref, o_ref,
                   xb_ref, wgb_ref, wub_ref, wdb_ref, acc_ref, *, bm, n_f, dm):
    del e_ref
    sb = pl.program_id(0)
    f = pl.program_id(1)
    n_rows = r_ref[sb]

    @pl.when(f == 0)
    def _():
        for c in range(ROW_CHUNKS):
            xb_ref[:, c * LANES:(c + 1) * LANES] = x_ref[pl.ds(c, bm, stride=ROW_CHUNKS), :].astype(BF16)
        acc_ref[...] = jnp.zeros_like(acc_ref)

    @pl.when(n_rows > 0)
    def _():
        wgb_ref[...] = wg_ref[...].astype(BF16)
        wub_ref[...] = wu_ref[...].astype(BF16)
        wdb_ref[...] = wd_ref[...].astype(BF16)

    for c in range(bm // MOE_CHUNK):
        @pl.when(n_rows > c * MOE_CHUNK)
        def _(c=c):
            rs = slice(c * MOE_CHUNK, (c + 1) * MOE_CHUNK)
            xb = xb_ref[rs, :]
            g = _dot(xb, wgb_ref[...])
            u = _dot(xb, wub_ref[...])
            hh = (g * _sigmoid(g) * u).astype(BF16)
            for n0 in range(0, dm, 4 * LANES):
                ns = slice(n0, n0 + 4 * LANES)
                acc_ref[rs, ns] += _dot(hh, wdb_ref[:, ns])

    @pl.when(f == n_f - 1)
    def _():
        for c in range(ROW_CHUNKS):
            o_ref[pl.ds(c, bm, stride=ROW_CHUNKS), :] = acc_ref[:, c * LANES:(c + 1) * LANES]


def _experts(buf, sb_expert, sb_rows, w_gate, w_up, w_down, layer):
    _, _, dm, d_exp = w_gate.shape
    bm = MOE_ROWS
    tf = MOE_FTILE
    n_f = d_exp // tf
    n_sb = sb_expert.shape[0]

    def f_idx(sb, f, r):
        return jnp.where(r[sb] > 0, f, n_f - 1)

    grid_spec = pltpu.PrefetchScalarGridSpec(
        num_scalar_prefetch=2,
        grid=(n_sb, n_f),
        in_specs=[
            pl.BlockSpec((bm * ROW_CHUNKS, LANES), lambda sb, f, e, r: (sb, 0)),
            pl.BlockSpec((None, None, dm, tf), lambda sb, f, e, r: (layer, e[sb], 0, f_idx(sb, f, r))),
            pl.BlockSpec((None, None, dm, tf), lambda sb, f, e, r: (layer, e[sb], 0, f_idx(sb, f, r))),
            pl.BlockSpec((None, None, tf, dm), lambda sb, f, e, r: (layer, e[sb], f_idx(sb, f, r), 0)),
        ],
        out_specs=pl.BlockSpec((bm * ROW_CHUNKS, LANES), lambda sb, f, e, r: (sb, 0)),
        scratch_shapes=[pltpu.VMEM((bm, dm), BF16), pltpu.VMEM((dm, tf), BF16), pltpu.VMEM((dm, tf), BF16),
                        pltpu.VMEM((tf, dm), BF16), pltpu.VMEM((bm, dm), F32)],
    )
    return pl.pallas_call(
        functools.partial(_expert_kernel, bm=bm, n_f=n_f, dm=dm),
        grid_spec=grid_spec,
        out_shape=jax.ShapeDtypeStruct(buf.shape, F32),
        compiler_params=_cparams(("arbitrary", "arbitrary")),
        name="moe_experts",
    )(sb_expert, sb_rows, buf, w_gate, w_up, w_down)


def _combine_kernel(pos_ref, y_ref, x_ref, w_ref, gate_ref, o_ref, ybuf_ref, sem, *, n_lat, tb, dm):
    i = pl.program_id(0)

    def row_copy(t, k):
        src = y_ref.at[pl.ds(pl.multiple_of(pos_ref[k, t], ROW_CHUNKS), ROW_CHUNKS), :]
        dst = ybuf_ref.at[k, pl.ds(pl.multiple_of(t * ROW_CHUNKS, ROW_CHUNKS), ROW_CHUNKS), :]
        return pltpu.make_async_copy(src, dst, sem.at[k])

    def issue(t, carry):
        row_copy(t, 0).start()
        row_copy(t, 1).start()
        return carry

    def drain(t, carry):
        row_copy(t, 0).wait()
        row_copy(t, 1).wait()
        return carry

    lax.fori_loop(0, tb, issue, 0)
    lax.fori_loop(0, tb, drain, 0)

    row = i * tb + lax.broadcasted_iota(jnp.int32, (tb, 1), 0)
    is_ctx = row >= n_lat
    w0 = w_ref[:, 0:1]
    w1 = w_ref[:, 1:2]
    for c in range(dm // LANES):
        cs = slice(c * LANES, (c + 1) * LANES)
        y0 = ybuf_ref[0, pl.ds(c, tb, stride=ROW_CHUNKS), :]
        y1 = ybuf_ref[1, pl.ds(c, tb, stride=ROW_CHUNKS), :]
        gate = jnp.where(is_ctx, gate_ref[1:2, cs], gate_ref[0:1, cs])
        o_ref[:, cs] = x_ref[:, cs] + gate * (w0 * y0 + w1 * y1)


def _combine(y, pos16, wcol, xs, mod_l, n_lat):
    n_rows, dm = xs.shape
    tb = _row_tile(n_rows, 640, step=LANES)
    gate_blk = 5
    return pl.pallas_call(
        functools.partial(_combine_kernel, n_lat=n_lat, tb=tb, dm=dm),
        grid=(n_rows // tb,),
        in_specs=[pl.BlockSpec((2, tb), lambda i: (0, i), memory_space=pltpu.SMEM),
                  pl.BlockSpec(memory_space=pl.ANY),
                  pl.BlockSpec((tb, dm), lambda i: (i, 0)),
                  pl.BlockSpec((tb, 2), lambda i: (i, 0)),
                  pl.BlockSpec((8, dm), lambda i: (0, gate_blk))],
        out_specs=pl.BlockSpec((tb, dm), lambda i: (i, 0)),
        out_shape=jax.ShapeDtypeStruct((n_rows, dm), F32),
        scratch_shapes=[pltpu.VMEM((2, tb * ROW_CHUNKS, LANES), F32), pltpu.SemaphoreType.DMA((2,))],
        compiler_params=_cparams(("arbitrary",)),
        name="moe_combine",
    )(pos16, y, xs, wcol, mod_l)


def _moe(xs, g2, mod_l, router_w, router_b, w_gate, w_up, w_down, layer, n_lat):
    n_rows = xs.shape[0]
    n_e = router_w.shape[1]
    bm = MOE_ROWS
    tok, eidx, wgt, rank, cnt = _router(xs, g2, mod_l, router_w, router_b, n_lat)
    counts = cnt[:, 0]
    n_sb = -(-2 * n_rows // bm) + n_e
    pcounts = (counts + bm - 1) // bm * bm
    pends = jnp.cumsum(pcounts)
    pstarts = pends - pcounts
    onehot = eidx[:, :, None] == jnp.arange(n_e, dtype=jnp.int32)[None, None, :]
    pos = jnp.sum(jnp.where(onehot, pstarts[None, None, :], 0), axis=-1) + rank
    pos16 = (pos * ROW_CHUNKS).astype(jnp.int32)
    sb_start = jnp.arange(n_sb, dtype=jnp.int32) * bm
    sb_expert = jnp.minimum(jnp.searchsorted(pends, sb_start, side='right'), n_e - 1).astype(jnp.int32)
    sb_rows = jnp.clip(counts[sb_expert] - (sb_start - pstarts[sb_expert]), 0, bm).astype(jnp.int32)
    buf = _dispatch(tok, pos16, n_sb * bm)
    y = _experts(buf, sb_expert, sb_rows, w_gate, w_up, w_down, layer)
    return _combine(y, pos16, wgt.T, xs, mod_l, n_lat)


def kernel(x, c, ctx, c_ctx, ada_w, ada_b, norm1_g, norm2_g, w_in, w_out, rg_conv_w, rg_conv_b,
           rg_gate_w, rg_gate_b, rg_lambda, na_qk_g, na_rpb, da_qk_g, da_lambda, da_subln_g,
           router_w, router_b, moe_w_gate, moe_w_up, moe_w_down):
    bsz, n_lat, dm = x.shape
    n_ctx = ctx.shape[1]
    assert bsz == 1 and c.shape[0] == 1
    assert n_ctx == SEQ_BLOCK and n_lat % SEQ_BLOCK == 0 and n_lat // GRID_W >= NA_WIN_H
    depth = ada_w.shape[0]
    rg_width = rg_conv_w.shape[2]
    na_heads = na_rpb.shape[1]
    na_width = na_heads * NA_HEAD_DIM
    da_width = da_subln_g.shape[1] * (dm // 4 // DA_V_DIM)
    da_heads = da_width // DA_V_DIM
    assert na_width == da_width == 512 and rg_width % na_width == 0
    na_q0 = 2 * rg_width
    qk_cols = [(na_q0 + k * na_width) // na_width for k in (0, 1, 3, 4)]
    na_v_col0 = (na_q0 + 2 * na_width) // NA_HEAD_DIM
    da_v_col0 = (na_q0 + 5 * na_width) // DA_V_DIM

    xs = jnp.concatenate([x[0], ctx[0]], axis=0)
    cvec_t = jnp.stack([c[0], c_ctx], axis=1)
    mod = _ada(cvec_t, ada_w, ada_b)
    rope_tabs = _rope_tables(n_lat, n_ctx)
    w_in_b = w_in.astype(BF16)
    w_out_b = w_out.astype(BF16)

    for l in range(depth):
        lambda_init = 0.8 - 0.6 * math.exp(-0.3 * l)
        mod_l = mod[l]
        p = _inproj(xs, norm1_g[l], mod_l, w_in_b[l], n_lat)
        ra = _rg_group(p, rg_conv_w[l], rg_conv_b[l], rg_gate_w[l], rg_gate_b[l], rg_lambda[l], n_lat)
        nq, nk, dqq, dk = _qkprep(p, na_qk_g[l], da_qk_g[l], rope_tabs, qk_cols, na_width)
        bias = _na_bias(na_rpb[l], n_lat // GRID_W)
        nb = _na_attention(nq, nk, p, bias, na_v_col0, n_lat, na_heads)
        da = _da_attention(dqq, dk, p, da_lambda[l], da_subln_g[l], da_v_col0, n_lat, da_heads, lambda_init)
        xs = _outproj(ra, nb, da, w_out_b[l], xs, mod_l, n_lat)
        xs = _moe(xs, norm2_g[l], mod_l, router_w, router_b, moe_w_gate, moe_w_up, moe_w_down, l, n_lat)
    return xs[:n_lat][None]
```

```python
import functools
import math

import numpy as np
import jax
import jax.numpy as jnp
from jax import lax
from jax.experimental import pallas as pl
from jax.experimental.pallas import tpu as pltpu

F32 = jnp.float32
BF16 = jnp.bfloat16

EPS = 1e-6
GRID_W = 64
RG_BLOCKS = 8
RG_C = 8.0
NA_HEAD_DIM = 128
NA_WIN_H = 8
NA_WIN_W = 16
DA_HEAD_DIM = 64
DA_V_DIM = 128
ROPE_THETA = 10000.0
N_EXPERTS = 16
N_GROUPS = 4
GROUP_SIZE = 4

LANES = 128
VMEM_LIMIT_BYTES = 56 * 1024 * 1024
SEQ_BLOCK = 256
MOE_ROWS = 512
MOE_CHUNK = 256
MOE_FTILE = 256
ROW_CHUNKS = 16
NORM_CHUNK = 128
MM_CHUNK = 256
DA_UNROLL = 4
NEG_BIG = -1e30


def _cparams(sem):
    return pltpu.CompilerParams(dimension_semantics=sem, vmem_limit_bytes=VMEM_LIMIT_BYTES)


def _dot(a, b):
    return jnp.dot(a, b, preferred_element_type=F32)


def _dot_t(a, b):
    return lax.dot_general(a, b, (((1,), (1,)), ((), ())), preferred_element_type=F32)


def _row_tile(n_rows, cap, step=MM_CHUNK):
    best = None
    for t in range(step, cap + 1, step):
        if n_rows % t == 0:
            best = t
    assert best is not None, n_rows
    return best


def _sigmoid(z):
    return 1.0 / (1.0 + jnp.exp(-z))


def _row_loop(n_rows, chunk, body):
    assert n_rows % chunk == 0

    def step(i, carry):
        body(pl.multiple_of(i * chunk, chunk))
        return carry

    lax.fori_loop(0, n_rows // chunk, step, 0)


def _ada_kernel(a_ref, w_ref, b_ref, o_ref):
    chunk = 256
    tn = w_ref.shape[1]

    def step(i, carry):
        r0, r1 = carry
        k0 = pl.multiple_of(i * chunk, chunk)
        a = a_ref[pl.ds(k0, chunk), :]
        act = a * _sigmoid(a)
        w = w_ref[pl.ds(k0, chunk), :]
        r0 = r0 + jnp.sum(w * act[:, 0:1], axis=0, keepdims=True)
        r1 = r1 + jnp.sum(w * act[:, 1:2], axis=0, keepdims=True)
        return r0, r1

    r0, r1 = lax.fori_loop(0, w_ref.shape[0] // chunk, step, (b_ref[...], b_ref[...]))
    o_ref[...] = jnp.concatenate([r0, r1, jnp.zeros((6, tn), F32)], axis=0)


def _ada(cvec_t, ada_w, ada_b):
    depth, dm, n6 = ada_w.shape
    tn = 1024
    return pl.pallas_call(
        _ada_kernel,
        grid=(depth, n6 // tn),
        in_specs=[
            pl.BlockSpec((dm, 2), lambda l, j: (0, 0)),
            pl.BlockSpec((None, dm, tn), lambda l, j: (l, 0, j)),
            pl.BlockSpec((None, 1, tn), lambda l, j: (l, 0, j)),
        ],
        out_specs=pl.BlockSpec((None, 8, tn), lambda l, j: (l, 0, j)),
        out_shape=jax.ShapeDtypeStruct((depth, 8, n6), F32),
        compiler_params=_cparams(("parallel", "parallel")),
        name="ada_mod",
    )(cvec_t, ada_w, ada_b.reshape(depth, 1, n6))


def _mod_rows(mod_ref, col0, width, is_ctx):
    return jnp.where(is_ctx, mod_ref[1:2, col0:col0 + width], mod_ref[0:1, col0:col0 + width])


def _norm_modulate(x, g, mod_ref, shift_col, scale_col, row0, n_lat, dm):
    y = x * lax.rsqrt(jnp.mean(x * x, axis=-1, keepdims=True) + EPS) * g
    row = row0 + lax.broadcasted_iota(jnp.int32, (x.shape[0], 1), 0)
    is_ctx = row >= n_lat
    return y * (1.0 + _mod_rows(mod_ref, scale_col, dm, is_ctx)) + _mod_rows(mod_ref, shift_col, dm, is_ctx)


def _inproj_kernel(x_ref, g_ref, mod_ref, w_ref, o_ref, h_ref, *, n_lat, tm, dm):
    i = pl.program_id(0)

    @pl.when(pl.program_id(1) == 0)
    def _():
        def norm_rows(r0):
            rs = pl.ds(r0, NORM_CHUNK)
            h_ref[rs, :] = _norm_modulate(x_ref[rs, :], g_ref[...], mod_ref, 0, dm, i * tm + r0,
                                          n_lat, dm).astype(BF16)
        _row_loop(tm, NORM_CHUNK, norm_rows)

    def mm_rows(r0):
        rs = pl.ds(r0, MM_CHUNK)
        o_ref[rs, :] = _dot(h_ref[rs, :], w_ref[...]).astype(BF16)
    _row_loop(tm, MM_CHUNK, mm_rows)


def _inproj(xs, g, mod_l, w_b, n_lat):
    n_rows, dm = xs.shape
    n_out = w_b.shape[1]
    tm = _row_tile(n_rows, 1280)
    tn = 1024
    return pl.pallas_call(
        functools.partial(_inproj_kernel, n_lat=n_lat, tm=tm, dm=dm),
        grid=(n_rows // tm, n_out // tn),
        in_specs=[
            pl.BlockSpec((tm, dm), lambda i, j: (i, 0)),
            pl.BlockSpec((1, dm), lambda i, j: (0, 0)),
            pl.BlockSpec(mod_l.shape, lambda i, j: (0, 0)),
            pl.BlockSpec((dm, tn), lambda i, j: (0, j)),
        ],
        out_specs=pl.BlockSpec((tm, tn), lambda i, j: (i, j)),
        out_shape=jax.ShapeDtypeStruct((n_rows, n_out), BF16),
        scratch_shapes=[pltpu.VMEM((tm, dm), BF16)],
        compiler_params=_cparams(("parallel", "arbitrary")),
        name="in_proj",
    )(xs, g.reshape(1, dm), mod_l, w_b)


def _gelu_tanh(g):
    return 0.5 * g * (1.0 + jnp.tanh(0.7978845608028654 * (g + 0.044715 * g * g * g)))


def _rg_kernel(*refs, reverse, n_blk_lat, t_blk, width):
    if reverse:
        (xc_ref, xp_ref, xn_ref, cw_ref, cb_ref, gw_ref, gb_ref, lam_ref,
         hf_ref, gate_ref, o_ref, carry_ref) = refs
    else:
        (xc_ref, xp_ref, xn_ref, cw_ref, cb_ref, gw_ref, gb_ref, lam_ref,
         o_ref, carry_ref) = refs
    s = pl.program_id(0)
    if reverse:
        blk = jnp.where(s == 0, n_blk_lat, n_blk_lat - s)
    else:
        blk = jnp.where(s == 0, n_blk_lat, s - 1)
    seq_first = jnp.logical_or(blk == 0, blk == n_blk_lat)
    seq_last = jnp.logical_or(blk == n_blk_lat - 1, blk == n_blk_lat)
    @pl.when(s == 0)
    def _():
        carry_ref[...] = jnp.zeros_like(carry_ref)

    row = lax.broadcasted_iota(jnp.int32, (t_blk, LANES), 0)
    for n in range(width // LANES):
        cs = slice(n * LANES, (n + 1) * LANES)
        x = xc_ref[:, cs].astype(F32)
        xp = jnp.where(seq_first, 0.0, xp_ref[:, cs].astype(F32))
        xn = jnp.where(seq_last, 0.0, xn_ref[:, cs].astype(F32))
        xe = jnp.concatenate([xp, x, xn], axis=0)
        cw = cw_ref[:, cs]
        u = (cw[0:1] * xe[6:6 + t_blk] + cw[1:2] * xe[7:7 + t_blk]
             + cw[2:3] * xe[8:8 + t_blk] + cw[3:4] * xe[9:9 + t_blk] + cb_ref[:, cs])
        z = _dot(u.astype(BF16), gw_ref[n]) + gb_ref[n]
        r = _sigmoid(z[:, :LANES])
        gi = _sigmoid(z[:, LANES:])
        zl = -lam_ref[:, cs]
        softplus = jnp.maximum(zl, 0.0) + jnp.log(1.0 + jnp.exp(-jnp.abs(zl)))
        a = jnp.exp(-RG_C * r * softplus)
        b = jnp.sqrt(1.0 - a * a) * (gi * u)
        d = 1
        while d < t_blk:
            if reverse:
                a_s = pltpu.roll(a, t_blk - d, 0)
                b_s = pltpu.roll(b, t_blk - d, 0)
                m = row < t_blk - d
            else:
                a_s = pltpu.roll(a, d, 0)
                b_s = pltpu.roll(b, d, 0)
                m = row >= d
            b = jnp.where(m, a * b_s + b, b)
            a = jnp.where(m, a * a_s, a)
            d *= 2
        h = a * carry_ref[:, cs] + b
        carry_ref[:, cs] = h[0:1] if reverse else h[t_blk - 1:t_blk]
        if reverse:
            o_ref[:, cs] = ((hf_ref[:, cs] + h) * _gelu_tanh(gate_ref[:, cs].astype(F32))).astype(BF16)
        else:
            o_ref[:, cs] = h


def _rg_direction(p, cw, cb, gw, gb, lam, hf, *, reverse, n_lat, width):
    n_rows = p.shape[0]
    t_blk = SEQ_BLOCK
    n_blk_lat = n_lat // t_blk
    n_steps = n_rows // t_blk
    halo_blocks = n_rows // 8
    per = t_blk // 8

    if reverse:
        def blk_of(s):
            return jnp.where(s == 0, n_blk_lat, n_blk_lat - s)
    else:
        def blk_of(s):
            return jnp.where(s == 0, n_blk_lat, s - 1)

    in_specs = [
        pl.BlockSpec((t_blk, width), lambda s: (blk_of(s), 0)),
        pl.BlockSpec((8, width), lambda s: (jnp.maximum(blk_of(s) * per - 1, 0), 0)),
        pl.BlockSpec((8, width), lambda s: (jnp.minimum((blk_of(s) + 1) * per, halo_blocks - 1), 0)),
        pl.BlockSpec((4, width), lambda s: (0, 0)),
        pl.BlockSpec((1, width), lambda s: (0, 0)),
        pl.BlockSpec((RG_BLOCKS, LANES, 2 * LANES), lambda s: (0, 0, 0)),
        pl.BlockSpec((RG_BLOCKS, 1, 2 * LANES), lambda s: (0, 0, 0)),
        pl.BlockSpec((1, width), lambda s: (0, 0)),
    ]
    args = [p, p, p, cw, cb, gw, gb, lam]
    if reverse:
        in_specs += [
            pl.BlockSpec((t_blk, width), lambda s: (blk_of(s), 0)),
            pl.BlockSpec((t_blk, width), lambda s: (blk_of(s), 1)),
        ]
        args += [hf, p]
        out_dtype = BF16
    else:
        out_dtype = F32
    return pl.pallas_call(
        functools.partial(_rg_kernel, reverse=reverse, n_blk_lat=n_blk_lat, t_blk=t_blk, width=width),
        grid=(n_steps,),
        in_specs=in_specs,
        out_specs=pl.BlockSpec((t_blk, width), lambda s: (blk_of(s), 0)),
        out_shape=jax.ShapeDtypeStruct((n_rows, width), out_dtype),
        scratch_shapes=[pltpu.VMEM((1, width), F32)],
        compiler_params=_cparams(("arbitrary",)),
        name="rg_lru_bwd" if reverse else "rg_lru_fwd",
    )(*args)


def _rg_group(p, conv_w, conv_b, gate_w, gate_b, lam, n_lat):
    width = conv_w.shape[1]
    outs = None
    for d in range(2):
        gw = jnp.concatenate([gate_w[d, 0], gate_w[d, 1]], axis=-1).astype(BF16)
        gb = jnp.concatenate([gate_b[d, 0].reshape(RG_BLOCKS, 1, LANES),
                              gate_b[d, 1].reshape(RG_BLOCKS, 1, LANES)], axis=-1)
        outs = _rg_direction(p, conv_w, conv_b.reshape(1, width), gw, gb, lam[d].reshape(1, width),
                             outs, reverse=(d == 1), n_lat=n_lat, width=width)
    return outs


def _group_norm(x, bd, gain, group):
    sq = x * x
    hi = sq.astype(BF16)
    lo = (sq - hi.astype(F32)).astype(BF16)
    ss = _dot(hi, bd) + _dot(lo, bd)
    return x * lax.rsqrt(ss * (1.0 / group) + EPS) * gain


def _qkprep_kernel(nq_ref, nk_ref, dq_ref, dk_ref, gains_ref, bd128_ref, bd64_ref,
                   cos_ref, sa_ref, sb_ref, onq_ref, onk_ref, odq_ref, odk_ref,
                   *, na_scale, da_scale):
    tm, w = nq_ref.shape
    reps = w // LANES

    def rows(r0):
        rs = pl.ds(r0, NORM_CHUNK)
        gains = gains_ref[...]
        onq_ref[rs, :] = (_group_norm(nq_ref[rs, :].astype(F32), bd128_ref[...], gains[0:1], NA_HEAD_DIM)
                          * na_scale).astype(BF16)
        onk_ref[rs, :] = _group_norm(nk_ref[rs, :].astype(F32), bd128_ref[...], gains[1:2],
                                     NA_HEAD_DIM).astype(BF16)
        cos = jnp.concatenate([cos_ref[rs, :]] * reps, axis=1)
        sa = jnp.concatenate([sa_ref[rs, :]] * reps, axis=1)
        sb = jnp.concatenate([sb_ref[rs, :]] * reps, axis=1)

        def rope(y):
            return y * cos + pltpu.roll(y, 16, 1) * sa + pltpu.roll(y, w - 16, 1) * sb

        q = rope(_group_norm(dq_ref[rs, :].astype(F32), bd64_ref[...], gains[2:3], DA_HEAD_DIM)) * da_scale
        k = rope(_group_norm(dk_ref[rs, :].astype(F32), bd64_ref[...], gains[3:4], DA_HEAD_DIM))
        lane = lax.broadcasted_iota(jnp.int32, q.shape, 1)
        first_map = (lane % LANES) < DA_HEAD_DIM
        odq_ref[0, rs, :] = jnp.where(first_map, q, 0.0).astype(BF16)
        odq_ref[1, rs, :] = jnp.where(first_map, 0.0, q).astype(BF16)
        odk_ref[rs, :] = k.astype(BF16)

    _row_loop(tm, NORM_CHUNK, rows)


def _qkprep(p, na_g, da_g, rope_tabs, col_blocks, w):
    n_rows = p.shape[0]
    tm = _row_tile(n_rows, 1280)
    gains = jnp.stack([jnp.tile(na_g[0], w // NA_HEAD_DIM), jnp.tile(na_g[1], w // NA_HEAD_DIM),
                       jnp.tile(da_g[0], w // DA_HEAD_DIM), jnp.tile(da_g[1], w // DA_HEAD_DIM)])
    lane = np.arange(w)
    bd128 = jnp.asarray((lane[:, None] // NA_HEAD_DIM) == (lane[None, :] // NA_HEAD_DIM), BF16)
    bd64 = jnp.asarray((lane[:, None] // DA_HEAD_DIM) == (lane[None, :] // DA_HEAD_DIM), BF16)
    cos, sa, sb = rope_tabs
    row_blk = lambda c: pl.BlockSpec((tm, w), lambda i, c=c: (i, c))
    const = lambda shape: pl.BlockSpec(shape, lambda i: (0,) * len(shape))
    tab = pl.BlockSpec((tm, LANES), lambda i: (i, 0))
    return pl.pallas_call(
        functools.partial(_qkprep_kernel, na_scale=NA_HEAD_DIM ** -0.5,
                          da_scale=DA_HEAD_DIM ** -0.5 * math.log2(math.e)),
        grid=(n_rows // tm,),
        in_specs=[row_blk(col_blocks[0]), row_blk(col_blocks[1]), row_blk(col_blocks[2]),
                  row_blk(col_blocks[3]), const((4, w)), const((w, w)), const((w, w)), tab, tab, tab],
        out_specs=[pl.BlockSpec((tm, w), lambda i: (i, 0)), pl.BlockSpec((tm, w), lambda i: (i, 0)),
                   pl.BlockSpec((2, tm, w), lambda i: (0, i, 0)), pl.BlockSpec((tm, w), lambda i: (i, 0))],
        out_shape=[jax.ShapeDtypeStruct((n_rows, w), BF16), jax.ShapeDtypeStruct((n_rows, w), BF16),
                   jax.ShapeDtypeStruct((2, n_rows, w), BF16), jax.ShapeDtypeStruct((n_rows, w), BF16)],
        compiler_params=_cparams(("parallel",)),
        name="qk_prep",
    )(p, p, p, p, gains, bd128, bd64, cos, sa, sb)


def _rope_tables(n_lat, n_ctx):
    t = jnp.arange(n_lat, dtype=jnp.int32)
    n = DA_HEAD_DIM // 4
    inv = 1.0 / (ROPE_THETA ** (jnp.arange(n, dtype=F32) / n))
    ang_r = (t // GRID_W).astype(F32)[:, None] * inv
    ang_c = (t % GRID_W).astype(F32)[:, None] * inv
    cr, sr, cc, sc = jnp.cos(ang_r), jnp.sin(ang_r), jnp.cos(ang_c), jnp.sin(ang_c)
    z = jnp.zeros_like(sr)
    cos = jnp.concatenate([cr, cr, cc, cc], axis=1)
    sa = jnp.concatenate([z, sr, z, sc], axis=1)
    sb = jnp.concatenate([-sr, z, -sc, z], axis=1)

    def full(tab, ctx_val):
        tab = jnp.concatenate([tab, tab], axis=1)
        return jnp.concatenate([tab, jnp.full((n_ctx, LANES), ctx_val, F32)], axis=0)

    return full(cos, 1.0), full(sa, 0.0), full(sb, 0.0)


NA_QROWS = SEQ_BLOCK // GRID_W


def _na_bias(rpb, rows):
    n_blk = rows // NA_QROWS
    n_key = 3 * NA_QROWS
    idx_r = np.zeros((3, NA_QROWS, n_key), np.int32)
    val_r = np.zeros((3, NA_QROWS, n_key), bool)
    for v, b in enumerate((0, min(1, n_blk - 1), n_blk - 1)):
        for i in range(NA_QROWS):
            r = NA_QROWS * b + i
            rs = min(max(r - NA_WIN_H // 2, 0), rows - NA_WIN_H)
            for j in range(n_key):
                kr = NA_QROWS * (b - 1) + j
                val_r[v, i, j] = rs <= kr < rs + NA_WIN_H
                idx_r[v, i, j] = min(max(kr - r + NA_WIN_H - 1, 0), 2 * NA_WIN_H - 2)
    col = np.arange(GRID_W)
    cstart = np.clip(col - NA_WIN_W // 2, 0, GRID_W - NA_WIN_W)
    val_c = (col[None, :] >= cstart[:, None]) & (col[None, :] < cstart[:, None] + NA_WIN_W)
    idx_c = np.clip(col[None, :] - col[:, None] + NA_WIN_W - 1, 0, 2 * NA_WIN_W - 2)
    heads = rpb.shape[0]
    bcol = jnp.where(val_c[None, None], rpb[:, :, idx_c], NEG_BIG)
    big = bcol[:, idx_r]
    big = jnp.where(val_r[None, :, :, :, None, None], big, NEG_BIG)
    big = jnp.transpose(big, (1, 0, 2, 4, 3, 5))
    return big.reshape(3, heads, NA_QROWS * GRID_W, n_key * GRID_W).astype(F32)


def _na_kernel(q_ref, kp_ref, kc_ref, kn_ref, kx_ref, vp_ref, vc_ref, vn_ref, vx_ref, bias_ref,
               o_ref, *, n_blk_lat):
    b = pl.program_id(1)
    t = SEQ_BLOCK
    q = q_ref[...]
    sx = _dot_t(q, kx_ref[...])

    @pl.when(b < n_blk_lat)
    def _():
        s0 = _dot_t(q, kp_ref[...]) + bias_ref[:, 0:t]
        s1 = _dot_t(q, kc_ref[...]) + bias_ref[:, t:2 * t]
        s2 = _dot_t(q, kn_ref[...]) + bias_ref[:, 2 * t:3 * t]
        m = jnp.maximum(jnp.maximum(jnp.max(s0, -1, keepdims=True), jnp.max(s1, -1, keepdims=True)),
                        jnp.maximum(jnp.max(s2, -1, keepdims=True), jnp.max(sx, -1, keepdims=True)))
        p0, p1, p2, px = jnp.exp(s0 - m), jnp.exp(s1 - m), jnp.exp(s2 - m), jnp.exp(sx - m)
        l = (jnp.sum(p0, -1, keepdims=True) + jnp.sum(p1, -1, keepdims=True)
             + jnp.sum(p2, -1, keepdims=True) + jnp.sum(px, -1, keepdims=True))
        o = (_dot(p0.astype(BF16), vp_ref[...]) + _dot(p1.astype(BF16), vc_ref[...])
             + _dot(p2.astype(BF16), vn_ref[...]) + _dot(px.astype(BF16), vx_ref[...]))
        o_ref[...] = (o / l).astype(BF16)

    @pl.when(b == n_blk_lat)
    def _():
        m = jnp.max(sx, -1, keepdims=True)
        px = jnp.exp(sx - m)
        o = _dot(px.astype(BF16), vx_ref[...])
        o_ref[...] = (o / jnp.sum(px, -1, keepdims=True)).astype(BF16)


def _na_attention(nq, nk, p, bias, v_col0, n_lat, heads):
    n_rows = nq.shape[0]
    t = SEQ_BLOCK
    n_blk_lat = n_lat // t
    n_blk = n_rows // t
    last = n_blk_lat - 1

    def q_map(h, b):
        return (b, h)

    def prev_map(h, b):
        return (jnp.clip(b - 1, 0, last), h)

    def cur_map(h, b):
        return (jnp.minimum(b, last), h)

    def next_map(h, b):
        return (jnp.clip(b + 1, 0, last), h)

    def ctx_map(h, b):
        return (n_blk_lat, h)

    def voff(fn):
        return lambda h, b: (fn(h, b)[0], fn(h, b)[1] + v_col0)

    def var_map(h, b):
        return (jnp.where(b == 0, 0, jnp.where(b >= last, 2, 1)), h, 0, 0)

    blk = lambda fn: pl.BlockSpec((t, NA_HEAD_DIM), fn)
    return pl.pallas_call(
        functools.partial(_na_kernel, n_blk_lat=n_blk_lat),
        grid=(heads, n_blk),
        in_specs=[blk(q_map), blk(prev_map), blk(cur_map), blk(next_map), blk(ctx_map),
                  blk(voff(prev_map)), blk(voff(cur_map)), blk(voff(next_map)), blk(voff(ctx_map)),
                  pl.BlockSpec((None, None, t, 3 * t), var_map)],
        out_specs=pl.BlockSpec((t, NA_HEAD_DIM), q_map),
        out_shape=jax.ShapeDtypeStruct((n_rows, heads * NA_HEAD_DIM), BF16),
        compiler_params=_cparams(("parallel", "arbitrary")),
        name="na_attn",
    )(nq, nk, nk, nk, nk, p, p, p, p, bias)


def _da_kernel(q_ref, k_ref, v_ref, dl_ref, sg_ref, o_ref, vx_ref, m_ref, acc_ref,
               *, n_lat, n_ctx, tk, lambda_init):
    qb = pl.program_id(1)
    tq = q_ref.shape[1]
    dv = v_ref.shape[1]

    @pl.when(qb == 0)
    def _():
        def rows(r0):
            rs = pl.ds(r0, SEQ_BLOCK)
            vx_ref[rs, 0:dv] = v_ref[rs, :]
            vx_ref[rs, dv:2 * dv] = jnp.ones((SEQ_BLOCK, dv), BF16)
        _row_loop(v_ref.shape[0], SEQ_BLOCK, rows)

    m_ref[...] = jnp.full(m_ref.shape, -jnp.inf, F32)
    acc_ref[...] = jnp.zeros_like(acc_ref)

    def chunk(start, size):
        k = k_ref[pl.ds(start, size), :]
        vx = vx_ref[pl.ds(start, size), :]
        for mp in range(2):
            s = _dot_t(q_ref[mp], k)
            mx = s[:, 0:LANES]
            for j in range(1, size // LANES):
                mx = jnp.maximum(mx, s[:, j * LANES:(j + 1) * LANES])
            m_old = m_ref[mp]
            m_new = jnp.maximum(m_old, jnp.max(mx, -1, keepdims=True))
            alpha = jnp.exp2(m_old - m_new)
            p = jnp.exp2(s - m_new)
            acc_ref[mp] = alpha * acc_ref[mp] + _dot(p.astype(BF16), vx)
            m_ref[mp] = m_new

    @pl.when(qb < n_lat // tq)
    def _():
        def body(c, carry):
            chunk(pl.multiple_of(c * tk, tk), tk)
            return carry
        lax.fori_loop(0, n_lat // tk, body, 0, unroll=DA_UNROLL)

    chunk(n_lat, n_ctx)

    o1 = acc_ref[0, :, 0:dv] / acc_ref[0, :, dv:2 * dv]
    o2 = acc_ref[1, :, 0:dv] / acc_ref[1, :, dv:2 * dv]
    dl = dl_ref[...]
    lam = (jnp.exp(jnp.sum(dl[0:1] * dl[1:2], keepdims=True))
           - jnp.exp(jnp.sum(dl[2:3] * dl[3:4], keepdims=True)) + lambda_init)
    d = o1 - lam * o2
    y = d * lax.rsqrt(jnp.mean(d * d, axis=-1, keepdims=True) + EPS) * sg_ref[...]
    o_ref[...] = (y * (1.0 - lambda_init)).astype(BF16)


def _da_attention(dqq, dk, p, da_lam, subln_g, v_col0, n_lat, heads, lambda_init):
    n_rows = dk.shape[0]
    n_ctx = n_rows - n_lat
    tq = SEQ_BLOCK
    tk = 512 if n_lat % 512 == 0 else SEQ_BLOCK
    return pl.pallas_call(
        functools.partial(_da_kernel, n_lat=n_lat, n_ctx=n_ctx, tk=tk, lambda_init=lambda_init),
        grid=(heads, n_rows // tq),
        in_specs=[
            pl.BlockSpec((2, tq, DA_V_DIM), lambda h, b: (0, b, h)),
            pl.BlockSpec((n_rows, DA_V_DIM), lambda h, b: (0, h)),
            pl.BlockSpec((n_rows, DA_V_DIM), lambda h, b: (0, h + v_col0)),
            pl.BlockSpec((4, DA_HEAD_DIM), lambda h, b: (0, 0)),
            pl.BlockSpec((1, DA_V_DIM), lambda h, b: (0, 0)),
        ],
        out_specs=pl.BlockSpec((tq, DA_V_DIM), lambda h, b: (b, h)),
        out_shape=jax.ShapeDtypeStruct((n_rows, heads * DA_V_DIM), BF16),
        scratch_shapes=[pltpu.VMEM((n_rows, 2 * DA_V_DIM), BF16), pltpu.VMEM((2, tq, 1), F32),
                        pltpu.VMEM((2, tq, 2 * DA_V_DIM), F32)],
        compiler_params=_cparams(("arbitrary", "arbitrary")),
        name="da_attn",
    )(dqq, dk, p, da_lam, subln_g.reshape(1, DA_V_DIM))


def _outproj_kernel(ra_ref, nb_ref, da_ref, w_ref, x_ref, gate_ref, o_ref, *, n_lat, tm, k_ra, k_nb):
    i = pl.program_id(0)

    def rows(r0):
        rs = pl.ds(r0, MM_CHUNK)
        acc = (_dot(ra_ref[rs, :], w_ref[0:k_ra, :]) + _dot(nb_ref[rs, :], w_ref[k_ra:k_ra + k_nb, :])
               + _dot(da_ref[rs, :], w_ref[k_ra + k_nb:, :]))
        row = i * tm + r0 + lax.broadcasted_iota(jnp.int32, (MM_CHUNK, 1), 0)
        gate = jnp.where(row >= n_lat, gate_ref[1:2, :], gate_ref[0:1, :])
        o_ref[rs, :] = x_ref[rs, :] + gate * acc

    _row_loop(tm, MM_CHUNK, rows)


def _outproj(ra, nb, da, w_b, xs, mod_l, n_lat):
    n_rows, dm = xs.shape
    tm = _row_tile(n_rows, 1280)
    tn = 1024
    gate_blk0 = 2 * dm // tn
    return pl.pallas_call(
        functools.partial(_outproj_kernel, n_lat=n_lat, tm=tm, k_ra=ra.shape[1], k_nb=nb.shape[1]),
        grid=(n_rows // tm, dm // tn),
        in_specs=[
            pl.BlockSpec((tm, ra.shape[1]), lambda i, j: (i, 0)),
            pl.BlockSpec((tm, nb.shape[1]), lambda i, j: (i, 0)),
            pl.BlockSpec((tm, da.shape[1]), lambda i, j: (i, 0)),
            pl.BlockSpec((w_b.shape[0], tn), lambda i, j: (0, j)),
            pl.BlockSpec((tm, tn), lambda i, j: (i, j)),
            pl.BlockSpec((8, tn), lambda i, j: (0, gate_blk0 + j)),
        ],
        out_specs=pl.BlockSpec((tm, tn), lambda i, j: (i, j)),
        out_shape=jax.ShapeDtypeStruct((n_rows, dm), F32),
        compiler_params=_cparams(("parallel", "parallel")),
        name="out_proj",
    )(ra, nb, da, w_b, xs, mod_l)


def _split_bf16(v):
    hi = v.astype(BF16)
    return hi, (v - hi.astype(F32)).astype(BF16)


def _router_kernel(x_ref, g_ref, mod_ref, rw_ref, rb_ref, tri_ref,
                   tok_ref, eidx_ref, wgt_ref, rank_ref, cnt_ref, run_ref, logit_ref, *, n_lat, tm, dm):
    i = pl.program_id(0)

    @pl.when(i == 0)
    def _():
        run_ref[...] = jnp.zeros_like(run_ref)

    n_chunks = dm // LANES

    def rows(r0):
        rs = pl.ds(r0, NORM_CHUNK)
        h = _norm_modulate(x_ref[rs, :], g_ref[...], mod_ref, 3 * dm, 4 * dm, i * tm + r0, n_lat, dm)
        for c in range(n_chunks):
            tok_ref[pl.ds(r0 * n_chunks + c, NORM_CHUNK, stride=n_chunks), :] = h[:, c * LANES:(c + 1) * LANES]
        h_hi, h_lo = _split_bf16(h)
        w_hi, w_lo = _split_bf16(rw_ref[...])
        logit_ref[:, rs] = _dot_t(w_hi, h_hi) + _dot_t(w_hi, h_lo) + _dot_t(w_lo, h_hi)

    _row_loop(tm, NORM_CHUNK, rows)
    aff = _sigmoid(logit_ref[...])
    sel = aff + rb_ref[...]

    def rows(a, e):
        return a[e:e + 1, :]

    best_g = jnp.zeros((1, tm), jnp.int32)
    best_s = None
    for g in range(N_GROUPS):
        v = [rows(sel, g * GROUP_SIZE + k) for k in range(GROUP_SIZE)]
        sc = None
        for a in range(GROUP_SIZE):
            for b in range(a + 1, GROUP_SIZE):
                pair = v[a] + v[b]
                sc = pair if sc is None else jnp.maximum(sc, pair)
        if best_s is None:
            best_s = sc
        else:
            upd = sc > best_s
            best_g = jnp.where(upd, g, best_g)
            best_s = jnp.where(upd, sc, best_s)

    def pick(a, k):
        out = rows(a, k)
        for g in range(1, N_GROUPS):
            out = jnp.where(best_g == g, rows(a, g * GROUP_SIZE + k), out)
        return out

    cand = [pick(sel, k) for k in range(GROUP_SIZE)]
    cand_aff = [pick(aff, k) for k in range(GROUP_SIZE)]
    i1 = jnp.zeros((1, tm), jnp.int32)
    v1 = cand[0]
    for k in range(1, GROUP_SIZE):
        upd = cand[k] > v1
        i1 = jnp.where(upd, k, i1)
        v1 = jnp.where(upd, cand[k], v1)
    i2 = jnp.full((1, tm), -1, jnp.int32)
    v2 = jnp.full((1, tm), -jnp.inf, F32)
    for k in range(GROUP_SIZE):
        upd = jnp.logical_and(i1 != k, cand[k] > v2)
        i2 = jnp.where(upd, k, i2)
        v2 = jnp.where(upd, cand[k], v2)
    a1 = cand_aff[0]
    a2 = cand_aff[0]
    for k in range(1, GROUP_SIZE):
        a1 = jnp.where(i1 == k, cand_aff[k], a1)
        a2 = jnp.where(i2 == k, cand_aff[k], a2)
    e1 = best_g * GROUP_SIZE + i1
    e2 = best_g * GROUP_SIZE + i2
    den = a1 + a2
    eidx_ref[...] = jnp.concatenate([e1, e2], axis=0)
    wgt_ref[...] = jnp.concatenate([a1 / den, a2 / den], axis=0)

    eid = lax.broadcasted_iota(jnp.int32, (N_EXPERTS, tm), 0)
    member = jnp.logical_or(eid == e1, eid == e2)
    memf = jnp.where(member, 1.0, 0.0)
    before = _dot(memf.astype(BF16), tri_ref[...]) + run_ref[:, 0:1]
    r1 = jnp.sum(jnp.where(eid == e1, before, 0.0), axis=0, keepdims=True)
    r2 = jnp.sum(jnp.where(eid == e2, before, 0.0), axis=0, keepdims=True)
    rank_ref[...] = jnp.concatenate([r1, r2], axis=0).astype(jnp.int32)
    run_ref[...] = run_ref[...] + jnp.sum(memf, axis=1, keepdims=True)
    cnt_ref[...] = run_ref[...].astype(jnp.int32)


def _router(xs, g, mod_l, router_w, router_b, n_lat):
    n_rows, dm = xs.shape
    tm = _row_tile(n_rows, 1280)
    n_e = router_w.shape[1]
    tri = jnp.asarray(np.arange(tm)[:, None] < np.arange(tm)[None, :], BF16)
    const = lambda shape: pl.BlockSpec(shape, lambda i: (0,) * len(shape))
    pair = pl.BlockSpec((2, tm), lambda i: (0, i))
    return pl.pallas_call(
        functools.partial(_router_kernel, n_lat=n_lat, tm=tm, dm=dm),
        grid=(n_rows // tm,),
        in_specs=[pl.BlockSpec((tm, dm), lambda i: (i, 0)), const((1, dm)), const(mod_l.shape),
                  const((n_e, dm)), const((n_e, 1)), const((tm, tm))],
        out_specs=[pl.BlockSpec((tm * ROW_CHUNKS, LANES), lambda i: (i, 0)), pair, pair, pair,
                   const((n_e, LANES))],
        out_shape=[jax.ShapeDtypeStruct((n_rows * ROW_CHUNKS, LANES), F32),
                   jax.ShapeDtypeStruct((2, n_rows), jnp.int32),
                   jax.ShapeDtypeStruct((2, n_rows), F32),
                   jax.ShapeDtypeStruct((2, n_rows), jnp.int32),
                   jax.ShapeDtypeStruct((n_e, LANES), jnp.int32)],
        scratch_shapes=[pltpu.VMEM((n_e, LANES), F32), pltpu.VMEM((n_e, tm), F32)],
        compiler_params=_cparams(("arbitrary",)),
        name="router",
    )(xs, g.reshape(1, dm), mod_l, router_w.T, router_b.reshape(n_e, 1), tri)


def _dispatch_kernel(pos_ref, tok_ref, buf_in_ref, buf_ref, sem, *, tb):
    del buf_in_ref
    i = pl.program_id(0)

    def row_copy(t, k):
        src = tok_ref.at[pl.ds(pl.multiple_of(t * ROW_CHUNKS, ROW_CHUNKS), ROW_CHUNKS), :]
        dst = buf_ref.at[pl.ds(pl.multiple_of(pos_ref[k, t], ROW_CHUNKS), ROW_CHUNKS), :]
        return pltpu.make_async_copy(src, dst, sem)

    def issue(t, carry):
        row_copy(t, 0).start()
        row_copy(t, 1).start()
        return carry

    def drain(t, carry):
        row_copy(t, 0).wait()
        row_copy(t, 1).wait()
        return carry

    lax.fori_loop(0, tb, issue, 0)
    lax.fori_loop(0, tb, drain, 0)


def _dispatch(tok, pos16, n_slots):
    n_rows = pos16.shape[1]
    tb = _row_tile(n_rows, 1280)
    buf0 = jnp.zeros((n_slots * ROW_CHUNKS, LANES), F32)
    return pl.pallas_call(
        functools.partial(_dispatch_kernel, tb=tb),
        grid=(n_rows // tb,),
        in_specs=[pl.BlockSpec((2, tb), lambda i: (0, i), memory_space=pltpu.SMEM),
                  pl.BlockSpec((tb * ROW_CHUNKS, LANES), lambda i: (i, 0)),
                  pl.BlockSpec(memory_space=pl.ANY)],
        out_specs=pl.BlockSpec(memory_space=pl.ANY),
        out_shape=jax.ShapeDtypeStruct(buf0.shape, F32),
        scratch_shapes=[pltpu.SemaphoreType.DMA(())],
        input_output_aliases={2: 0},
        compiler_params=_cparams(("arbitrary",)),
        name="moe_dispatch",
    )(pos16, tok, buf0)


def _expert_kernel(e_ref, r_ref, x_ref, wg_ref, wu_ref, wd_ref, o_ref,
                   xb_ref, wgb_ref, wub_ref, wdb_ref, acc_ref, *, bm, n_f, dm):
    del e_ref
    sb = pl.program_id(0)
    f = pl.program_id(1)
    n_rows = r_ref[sb]

    @pl.when(f == 0)
    def _():
        for c in range(ROW_CHUNKS):
            xb_ref[:, c * LANES:(c + 1) * LANES] = x_ref[pl.ds(c, bm, stride=ROW_CHUNKS), :].astype(BF16)
        acc_ref[...] = jnp.zeros_like(acc_ref)

    @pl.when(n_rows > 0)
    def _():
        wgb_ref[...] = wg_ref[...].astype(BF16)
        wub_ref[...] = wu_ref[...].astype(BF16)
        wdb_ref[...] = wd_ref[...].astype(BF16)

    for c in range(bm // MOE_CHUNK):
        @pl.when(n_rows > c * MOE_CHUNK)
        def _(c=c):
            rs = slice(c * MOE_CHUNK, (c + 1) * MOE_CHUNK)
            xb = xb_ref[rs, :]
            g = _dot(xb, wgb_ref[...])
            u = _dot(xb, wub_ref[...])
            hh = (g * _sigmoid(g) * u).astype(BF16)
            for n0 in range(0, dm, 4 * LANES):
                ns = slice(n0, n0 + 4 * LANES)
                acc_ref[rs, ns] += _dot(hh, wdb_ref[:, ns])

    @pl.when(f == n_f - 1)
    def _():
        for c in range(ROW_CHUNKS):
            o_ref[pl.ds(c, bm, stride=ROW_CHUNKS), :] = acc_ref[:, c * LANES:(c + 1) * LANES]


def _experts(buf, sb_expert, sb_rows, w_gate, w_up, w_down, layer):
    _, _, dm, d_exp = w_gate.shape
    bm = MOE_ROWS
    tf = MOE_FTILE
    n_f = d_exp // tf
    n_sb = sb_expert.shape[0]

    def f_idx(sb, f, r):
        return jnp.where(r[sb] > 0, f, n_f - 1)

    grid_spec = pltpu.PrefetchScalarGridSpec(
        num_scalar_prefetch=2,
        grid=(n_sb, n_f),
        in_specs=[
            pl.BlockSpec((bm * ROW_CHUNKS, LANES), lambda sb, f, e, r: (sb, 0)),
            pl.BlockSpec((None, None, dm, tf), lambda sb, f, e, r: (layer, e[sb], 0, f_idx(sb, f, r))),
            pl.BlockSpec((None, None, dm, tf), lambda sb, f, e, r: (layer, e[sb], 0, f_idx(sb, f, r))),
            pl.BlockSpec((None, None, tf, dm), lambda sb, f, e, r: (layer, e[sb], f_idx(sb, f, r), 0)),
        ],
        out_specs=pl.BlockSpec((bm * ROW_CHUNKS, LANES), lambda sb, f, e, r: (sb, 0)),
        scratch_shapes=[pltpu.VMEM((bm, dm), BF16), pltpu.VMEM((dm, tf), BF16), pltpu.VMEM((dm, tf), BF16),
                        pltpu.VMEM((tf, dm), BF16), pltpu.VMEM((bm, dm), F32)],
    )
    return pl.pallas_call(
        functools.partial(_expert_kernel, bm=bm, n_f=n_f, dm=dm),
        grid_spec=grid_spec,
        out_shape=jax.ShapeDtypeStruct(buf.shape, F32),
        compiler_params=_cparams(("arbitrary", "arbitrary")),
        name="moe_experts",
    )(sb_expert, sb_rows, buf, w_gate, w_up, w_down)


def _combine_kernel(pos_ref, y_ref, x_ref, w_ref, gate_ref, o_ref, ybuf_ref, sem, *, n_lat, tb, dm):
    i = pl.program_id(0)

    def row_copy(t, k):
        src = y_ref.at[pl.ds(pl.multiple_of(pos_ref[k, t], ROW_CHUNKS), ROW_CHUNKS), :]
        dst = ybuf_ref.at[k, pl.ds(pl.multiple_of(t * ROW_CHUNKS, ROW_CHUNKS), ROW_CHUNKS), :]
        return pltpu.make_async_copy(src, dst, sem.at[k])

    def issue(t, carry):
        row_copy(t, 0).start()
        row_copy(t, 1).start()
        return carry

    def drain(t, carry):
        row_copy(t, 0).wait()
        row_copy(t, 1).wait()
        return carry

    lax.fori_loop(0, tb, issue, 0)
    lax.fori_loop(0, tb, drain, 0)

    row = i * tb + lax.broadcasted_iota(jnp.int32, (tb, 1), 0)
    is_ctx = row >= n_lat
    w0 = w_ref[:, 0:1]
    w1 = w_ref[:, 1:2]
    for c in range(dm // LANES):
        cs = slice(c * LANES, (c + 1) * LANES)
        y0 = ybuf_ref[0, pl.ds(c, tb, stride=ROW_CHUNKS), :]
        y1 = ybuf_ref[1, pl.ds(c, tb, stride=ROW_CHUNKS), :]
        gate = jnp.where(is_ctx, gate_ref[1:2, cs], gate_ref[0:1, cs])
        o_ref[:, cs] = x_ref[:, cs] + gate * (w0 * y0 + w1 * y1)


def _combine(y, pos16, wcol, xs, mod_l, n_lat):
    n_rows, dm = xs.shape
    tb = _row_tile(n_rows, 640, step=LANES)
    gate_blk = 5
    return pl.pallas_call(
        functools.partial(_combine_kernel, n_lat=n_lat, tb=tb, dm=dm),
        grid=(n_rows // tb,),
        in_specs=[pl.BlockSpec((2, tb), lambda i: (0, i), memory_space=pltpu.SMEM),
                  pl.BlockSpec(memory_space=pl.ANY),
                  pl.BlockSpec((tb, dm), lambda i: (i, 0)),
                  pl.BlockSpec((tb, 2), lambda i: (i, 0)),
                  pl.BlockSpec((8, dm), lambda i: (0, gate_blk))],
        out_specs=pl.BlockSpec((tb, dm), lambda i: (i, 0)),
        out_shape=jax.ShapeDtypeStruct((n_rows, dm), F32),
        scratch_shapes=[pltpu.VMEM((2, tb * ROW_CHUNKS, LANES), F32), pltpu.SemaphoreType.DMA((2,))],
        compiler_params=_cparams(("arbitrary",)),
        name="moe_combine",
    )(pos16, y, xs, wcol, mod_l)


def _moe(xs, g2, mod_l, router_w, router_b, w_gate, w_up, w_down, layer, n_lat):
    n_rows = xs.shape[0]
    n_e = router_w.shape[1]
    bm = MOE_ROWS
    tok, eidx, wgt, rank, cnt = _router(xs, g2, mod_l, router_w, router_b, n_lat)
    counts = cnt[:, 0]
    n_sb = -(-2 * n_rows // bm) + n_e
    pcounts = (counts + bm - 1) // bm * bm
    pends = jnp.cumsum(pcounts)
    pstarts = pends - pcounts
    onehot = eidx[:, :, None] == jnp.arange(n_e, dtype=jnp.int32)[None, None, :]
    pos = jnp.sum(jnp.where(onehot, pstarts[None, None, :], 0), axis=-1) + rank
    pos16 = (pos * ROW_CHUNKS).astype(jnp.int32)
    sb_start = jnp.arange(n_sb, dtype=jnp.int32) * bm
    sb_expert = jnp.minimum(jnp.searchsorted(pends, sb_start, side='right'), n_e - 1).astype(jnp.int32)
    sb_rows = jnp.clip(counts[sb_expert] - (sb_start - pstarts[sb_expert]), 0, bm).astype(jnp.int32)
    buf = _dispatch(tok, pos16, n_sb * bm)
    y = _experts(buf, sb_expert, sb_rows, w_gate, w_up, w_down, layer)
    return _combine(y, pos16, wgt.T, xs, mod_l, n_lat)


def kernel(x, c, ctx, c_ctx, ada_w, ada_b, norm1_g, norm2_g, w_in, w_out, rg_conv_w, rg_conv_b,
           rg_gate_w, rg_gate_b, rg_lambda, na_qk_g, na_rpb, da_qk_g, da_lambda, da_subln_g,
           router_w, router_b, moe_w_gate, moe_w_up, moe_w_down):
    bsz, n_lat, dm = x.shape
    n_ctx = ctx.shape[1]
    assert bsz == 1 and c.shape[0] == 1
    assert n_ctx == SEQ_BLOCK and n_lat % SEQ_BLOCK == 0 and n_lat // GRID_W >= NA_WIN_H
    depth = ada_w.shape[0]
    rg_width = rg_conv_w.shape[2]
    na_heads = na_rpb.shape[1]
    na_width = na_heads * NA_HEAD_DIM
    da_width = da_subln_g.shape[1] * (dm // 4 // DA_V_DIM)
    da_heads = da_width // DA_V_DIM
    assert na_width == da_width == 512 and rg_width % na_width == 0
    na_q0 = 2 * rg_width
    qk_cols = [(na_q0 + k * na_width) // na_width for k in (0, 1, 3, 4)]
    na_v_col0 = (na_q0 + 2 * na_width) // NA_HEAD_DIM
    da_v_col0 = (na_q0 + 5 * na_width) // DA_V_DIM

    xs = jnp.concatenate([x[0], ctx[0]], axis=0)
    cvec_t = jnp.stack([c[0], c_ctx], axis=1)
    mod = _ada(cvec_t, ada_w, ada_b)
    rope_tabs = _rope_tables(n_lat, n_ctx)
    w_in_b = w_in.astype(BF16)
    w_out_b = w_out.astype(BF16)

    for l in range(depth):
        lambda_init = 0.8 - 0.6 * math.exp(-0.3 * l)
        mod_l = mod[l]
        p = _inproj(xs, norm1_g[l], mod_l, w_in_b[l], n_lat)
        ra = _rg_group(p, rg_conv_w[l], rg_conv_b[l], rg_gate_w[l], rg_gate_b[l], rg_lambda[l], n_lat)
        nq, nk, dqq, dk = _qkprep(p, na_qk_g[l], da_qk_g[l], rope_tabs, qk_cols, na_width)
        bias = _na_bias(na_rpb[l], n_lat // GRID_W)
        nb = _na_attention(nq, nk, p, bias, na_v_col0, n_lat, na_heads)
        da = _da_attention(dqq, dk, p, da_lambda[l], da_subln_g[l], da_v_col0, n_lat, da_heads, lambda_init)
        xs = _outproj(ra, nb, da, w_out_b[l], xs, mod_l, n_lat)
        xs = _moe(xs, norm2_g[l], mod_l, router_w, router_b, moe_w_gate, moe_w_up, moe_w_down, l, n_lat)
    return xs[:n_lat][None]
```
